```python
import math
import jax
import jax.numpy as jnp
from jax import lax
import numpy as np

D_MODEL = 1024
BATCH = 8
SEQ = 2048
DEPTH = 4
DEC_BATCH = 128
DEC_SEQ = 4
PAST_LEN = 16384
PAGE_SIZE = 128

BR_WIDTH = D_MODEL
N_BRANCH = 4
SSD_HEAD_DIM = 64
SSD_HEADS = BR_WIDTH // SSD_HEAD_DIM
SSD_GROUPS = 2
SSD_STATE = 128
SSD_CONV = 4
SSD_CONV_DIM = BR_WIDTH + 2 * SSD_GROUPS * SSD_STATE
S5_GROUP = 16
S5_GROUPS = BR_WIDTH // S5_GROUP
S5_STATE = 64
ML_HEADS = 4
ML_HEAD_DIM = BR_WIDTH // ML_HEADS
MEM_TOKENS = 256
XA_HEADS = 4
XA_HEAD_DIM = BR_WIDTH // XA_HEADS
CHUNK = 64
EPS = 1e-6
IN_SIZES = (BR_WIDTH, SSD_CONV_DIM, SSD_HEADS, BR_WIDTH, BR_WIDTH, BR_WIDTH, BR_WIDTH, BR_WIDTH,
            ML_HEADS, ML_HEADS, BR_WIDTH, BR_WIDTH, BR_WIDTH, BR_WIDTH, N_BRANCH * D_MODEL)
D_IN = sum(IN_SIZES)

kernel_name = 'hybrid_ssd_s5_mlstm_memxattn_step'


def _rmsnorm(x, g):
    xf = x.astype(jnp.float32)
    y = xf * lax.rsqrt(jnp.mean(xf * xf, axis=-1, keepdims=True) + EPS)
    return (y * g.astype(jnp.float32)).astype(x.dtype)


def _chunk_len(t):
    return CHUNK if t % CHUNK == 0 else t


def _to_chunks(a, L):
    b, t = a.shape[:2]
    return jnp.moveaxis(a.reshape((b, t // L, L) + a.shape[2:]), 1, 0)


def _from_chunks(a):
    a = jnp.moveaxis(a, 0, 1)
    return a.reshape((a.shape[0], a.shape[1] * a.shape[2]) + a.shape[3:])


def _causal_dwconv(x, buf, w, b):
    xp = jnp.concatenate([buf.astype(jnp.float32), x], axis=1)
    y = lax.conv_general_dilated(xp, w.astype(jnp.float32)[:, None, :], window_strides=(1,), padding='VALID',
                                 dimension_numbers=('NWC', 'WIO', 'NWC'), feature_group_count=x.shape[-1])
    return y + b, xp[:, xp.shape[1] - (SSD_CONV - 1):]


def _ssd(xbc_raw, dt_raw, conv_buf, state0, conv_w, conv_b, dt_bias, a_log, d_skip):
    bsz, t = xbc_raw.shape[:2]
    hg = SSD_HEADS // SSD_GROUPS
    xbc, new_buf = _causal_dwconv(xbc_raw, conv_buf, conv_w, conv_b)
    xbc = jax.nn.silu(xbc)
    xs, bm, cm = jnp.split(xbc, [BR_WIDTH, BR_WIDTH + SSD_GROUPS * SSD_STATE], axis=-1)
    xs = xs.reshape(bsz, t, SSD_GROUPS, hg, SSD_HEAD_DIM)
    bm = bm.reshape(bsz, t, SSD_GROUPS, SSD_STATE)
    cm = cm.reshape(bsz, t, SSD_GROUPS, SSD_STATE)
    dt = jax.nn.softplus(dt_raw + dt_bias).reshape(bsz, t, SSD_GROUPS, hg)
    la = dt * (-jnp.exp(a_log.astype(jnp.float32))).reshape(SSD_GROUPS, hg)
    L = _chunk_len(t)
    causal = jnp.tril(jnp.ones((L, L), dtype=jnp.bool_))

    def body(s, inp):
        xc, bc, cc, dtc, lac = inp
        acs = jnp.cumsum(lac, axis=1)
        seg = acs[:, :, None] - acs[:, None, :]
        decay = jnp.exp(jnp.where(causal[None, :, :, None, None], seg, -jnp.inf))
        cb = jnp.einsum('blgn,bsgn->blsg', cc, bc)
        xdt = xc * dtc[..., None]
        y_intra = jnp.einsum('blsg,blsgh,bsghp->blghp', cb, decay, xdt)
        y_inter = jnp.einsum('blgn,bghpn,blgh->blghp', cc, s, jnp.exp(acs))
        tail = jnp.exp(acs[:, -1:] - acs)
        s_new = jnp.exp(acs[:, -1])[..., None, None] * s + jnp.einsum('blgn,blgh,blghp->bghpn', bc, tail, xdt)
        return s_new, y_intra + y_inter

    s0 = state0.astype(jnp.float32).reshape(bsz, SSD_GROUPS, hg, SSD_HEAD_DIM, SSD_STATE)
    s_fin, y = lax.scan(body, s0, tuple(_to_chunks(a, L) for a in (xs, bm, cm, dt, la)))
    y = _from_chunks(y) + d_skip.reshape(SSD_GROUPS, hg)[..., None] * xs
    return (y.reshape(bsz, t, BR_WIDTH), new_buf,
            s_fin.reshape(bsz, SSD_HEADS, SSD_HEAD_DIM, SSD_STATE))


def _s5(u, state0_re, state0_im, a_re, a_im, log_dt, b_re, b_im, c_re, c_im, d_skip):
    f32 = jnp.float32
    bsz, t = u.shape[:2]
    lam = lax.complex(a_re.astype(f32), a_im.astype(f32))
    dt = jnp.exp(log_dt.astype(f32))[:, None]
    lam_bar = jnp.exp(lam * dt)
    b_bar = ((lam_bar - 1.0) / lam)[..., None] * lax.complex(b_re.astype(f32), b_im.astype(f32))
    ug = u.reshape(bsz, t, S5_GROUPS, S5_GROUP)
    bu = jnp.einsum('gnc,btgc->btgn', b_bar, ug.astype(jnp.complex64))
    h0 = lax.complex(state0_re.astype(f32), state0_im.astype(f32))
    bu = bu.at[:, 0].add(lam_bar * h0)
    a_el = jnp.broadcast_to(lam_bar, (t,) + lam_bar.shape)

    def combine(e1, e2):
        a1, b1 = e1
        a2, b2 = e2
        return a1 * a2, a2 * b1 + b2

    h = jax.vmap(lambda bseq: lax.associative_scan(combine, (a_el, bseq))[1])(bu)
    c = lax.complex(c_re.astype(f32), c_im.astype(f32))
    y = jnp.einsum('gcn,btgn->btgc', c, h).real.reshape(bsz, t, BR_WIDTH) + d_skip * u
    return y, h[:, -1].real, h[:, -1].imag


def _mlstm(q, k, v, i_raw, f_raw, c0, n0, m0):
    t = q.shape[1]
    L = _chunk_len(t)
    k = k * (ML_HEAD_DIM ** -0.5)
    logf = jax.nn.log_sigmoid(f_raw)
    causal = jnp.tril(jnp.ones((L, L), dtype=jnp.bool_))

    def body(carry, inp):
        c, n, m = carry
        qc, kc, vc, ic, lfc = inp
        bcum = jnp.cumsum(lfc, axis=1)
        dmat = bcum[:, :, None] - bcum[:, None, :] + ic[:, None, :]
        dmat = jnp.where(causal[None, :, :, None], dmat, -jnp.inf)
        g = bcum + m[:, None]
        m_l = jnp.maximum(g, jnp.max(dmat, axis=2))
        w = jnp.exp(dmat - m_l[:, :, None])
        w_inter = jnp.exp(g - m_l)
        qk = jnp.einsum('blhd,bshd->blsh', qc, kc) * w
        num = jnp.einsum('blsh,bshd->blhd', qk, vc) + w_inter[..., None] * jnp.einsum('blhd,bhde->blhe', qc, c)
        den = jnp.sum(qk, axis=2) + w_inter * jnp.einsum('blhd,bhd->blh', qc, n)
        h = num / jnp.maximum(jnp.abs(den), jnp.exp(-m_l))[..., None]
        g_end = bcum[:, -1] + m
        d_end = bcum[:, -1:] - bcum + ic
        m_new = jnp.maximum(g_end, jnp.max(d_end, axis=1))
        w_end = jnp.exp(d_end - m_new[:, None])
        decay = jnp.exp(g_end - m_new)
        c_new = decay[..., None, None] * c + jnp.einsum('blh,blhd,blhe->bhde', w_end, kc, vc)
        n_new = decay[..., None] * n + jnp.einsum('blh,blhd->bhd', w_end, kc)
        return (c_new, n_new, m_new), h

    init = (c0.astype(jnp.float32), n0.astype(jnp.float32), m0.astype(jnp.float32))
    (c_f, n_f, m_f), h = lax.scan(body, init, tuple(_to_chunks(a, L) for a in (q, k, v, i_raw, logf)))
    return _from_chunks(h), c_f, n_f, m_f


def _mem_kv(mem, g, w_kv):
    b, m = mem.shape[:2]
    kv = jnp.matmul(_rmsnorm(mem, g), w_kv)
    mk, mv = jnp.split(kv, 2, axis=-1)
    return mk.reshape(b, m, XA_HEADS, XA_HEAD_DIM), mv.reshape(b, m, XA_HEADS, XA_HEAD_DIM)


def _mem_attn(q, mk, mv):
    s = jnp.einsum('bthd,bmhd->bhtm', q, mk).astype(jnp.float32) * (XA_HEAD_DIM ** -0.5)
    p = jax.nn.softmax(s, axis=-1)
    return jnp.einsum('bhtm,bmhd->bthd', p, mv.astype(jnp.float32))


def _layer(x, mem_k, mem_v, conv_buf, ssd_s, s5_re, s5_im, ml_c, ml_n, ml_m, lw):
    (norm_in, w_in, b_gate, b_igate, b_fgate, ssd_conv_w, ssd_conv_b, ssd_dt_bias, ssd_a_log, ssd_d,
     ssd_norm, s5_a_re, s5_a_im, s5_log_dt, s5_b_re, s5_b_im, s5_c_re, s5_c_im, s5_d, s5_glu_w, s5_glu_b,
     ml_norm, w_down, w_out) = lw
    f32 = jnp.float32
    bsz, t = x.shape[:2]
    h = _rmsnorm(x, norm_in)
    proj = jnp.matmul(h, w_in).astype(f32)
    offs = []
    acc = 0
    for size in IN_SIZES[:-1]:
        acc += size
        offs.append(acc)
    (z_ssd, xbc, dt_raw, u_s5, z_s5, q_ml, k_ml, v_ml, i_raw, f_raw, o_ml, z_ml, q_xa, z_xa,
     gate_raw) = jnp.split(proj, offs, axis=-1)

    y_a, conv_new, ssd_new = _ssd(xbc, dt_raw, conv_buf, ssd_s, ssd_conv_w, ssd_conv_b, ssd_dt_bias, ssd_a_log, ssd_d)
    y_a = _rmsnorm(y_a * jax.nn.silu(z_ssd), ssd_norm)
    y_b, s5_re_new, s5_im_new = _s5(u_s5, s5_re, s5_im, s5_a_re, s5_a_im, s5_log_dt, s5_b_re, s5_b_im,
                                    s5_c_re, s5_c_im, s5_d)
    y_b = jax.nn.gelu(y_b)
    y_b = y_b * jax.nn.sigmoid(jnp.matmul(y_b, s5_glu_w.astype(f32)) + s5_glu_b) * jax.nn.silu(z_s5)
    heads = lambda a: a.reshape(bsz, t, ML_HEADS, ML_HEAD_DIM)
    y_c, c_new, n_new, m_new = _mlstm(heads(q_ml), heads(k_ml), heads(v_ml), i_raw + b_igate, f_raw + b_fgate,
                                      ml_c, ml_n, ml_m)
    y_c = _rmsnorm(jax.nn.sigmoid(heads(o_ml)) * y_c, ml_norm).reshape(bsz, t, BR_WIDTH) * jax.nn.silu(z_ml)
    y_d = _mem_attn(q_xa.reshape(bsz, t, XA_HEADS, XA_HEAD_DIM), mem_k, mem_v).reshape(bsz, t, BR_WIDTH)
    y_d = y_d * jax.nn.silu(z_xa)

    branches = jnp.stack([y_a, y_b, y_c, y_d], axis=2).astype(x.dtype)
    down = jnp.einsum('btkw,kwd->btkd', branches, w_down)
    gates = jax.nn.sigmoid(gate_raw + b_gate).reshape(bsz, t, N_BRANCH, D_MODEL)
    merged = jnp.sum(gates * down, axis=2).astype(x.dtype)
    x = x + jnp.matmul(merged, w_out).astype(x.dtype)
    dt = x.dtype
    return (x, conv_new.astype(dt), ssd_new.astype(dt), s5_re_new.astype(dt), s5_im_new.astype(dt),
            c_new.astype(dt), n_new.astype(dt), m_new.astype(dt))


def setup_inputs(seed: int = 0) -> dict:
    key = jax.random.key(seed)
    ks = iter(jax.random.split(key, 64))
    f32 = jnp.float32

    def nrm(shape, scale):
        return scale * jax.random.normal(next(ks), shape, f32)

    def unif(shape, lo, hi):
        return jax.random.uniform(next(ks), shape, f32, lo, hi)

    dt_ssd = jnp.exp(unif((DEPTH, SSD_HEADS), math.log(1e-3), math.log(1e-1)))
    n_idx = jnp.arange(S5_STATE, dtype=f32)
    return {
        'x_prompt': nrm((BATCH, SEQ, D_MODEL), 1.0),
        'x_sample': nrm((DEC_BATCH, DEC_SEQ, D_MODEL), 1.0),
        'mem_prompt': nrm((BATCH, MEM_TOKENS, D_MODEL), 1.0),
        'cache_mem_k': nrm((DEPTH, DEC_BATCH, MEM_TOKENS, XA_HEADS, XA_HEAD_DIM), 1.0),
        'cache_mem_v': nrm((DEPTH, DEC_BATCH, MEM_TOKENS, XA_HEADS, XA_HEAD_DIM), 1.0),
        'state_ssd_conv': nrm((DEPTH, DEC_BATCH, SSD_CONV - 1, SSD_CONV_DIM), 1.0),
        'state_ssd': nrm((DEPTH, DEC_BATCH, SSD_HEADS, SSD_HEAD_DIM, SSD_STATE), 0.1),
        'state_s5_re': nrm((DEPTH, DEC_BATCH, S5_GROUPS, S5_STATE), 0.5),
        'state_s5_im': nrm((DEPTH, DEC_BATCH, S5_GROUPS, S5_STATE), 0.5),
        'state_mlstm_c': nrm((DEPTH, DEC_BATCH, ML_HEADS, ML_HEAD_DIM, ML_HEAD_DIM), 0.05),
        'state_mlstm_n': nrm((DEPTH, DEC_BATCH, ML_HEADS, ML_HEAD_DIM), 0.5),
        'state_mlstm_m': nrm((DEPTH, DEC_BATCH, ML_HEADS), 0.5),
        'norm_in': 1.0 + nrm((DEPTH, D_MODEL), 0.02),
        'w_in': nrm((DEPTH, D_MODEL, D_IN), D_MODEL ** -0.5),
        'b_gate': nrm((DEPTH, N_BRANCH * D_MODEL), 0.02),
        'b_igate': nrm((DEPTH, ML_HEADS), 0.1),
        'b_fgate': 3.0 + unif((DEPTH, ML_HEADS), 0.0, 3.0),
        'ssd_conv_w': nrm((DEPTH, SSD_CONV, SSD_CONV_DIM), SSD_CONV ** -0.5),
        'ssd_conv_b': nrm((DEPTH, SSD_CONV_DIM), 0.02),
        'ssd_dt_bias': dt_ssd + jnp.log(-jnp.expm1(-dt_ssd)),
        'ssd_a_log': jnp.log(unif((DEPTH, SSD_HEADS), 1.0, 16.0)),
        'ssd_d': 1.0 + nrm((DEPTH, SSD_HEADS), 0.1),
        'ssd_norm': 1.0 + nrm((DEPTH, BR_WIDTH), 0.02),
        's5_a_re': -0.5 + nrm((DEPTH, S5_GROUPS, S5_STATE), 0.01),
        's5_a_im': math.pi * n_idx + nrm((DEPTH, S5_GROUPS, S5_STATE), 0.01),
        's5_log_dt': unif((DEPTH, S5_GROUPS), math.log(1e-3), math.log(1e-1)),
        's5_b_re': nrm((DEPTH, S5_GROUPS, S5_STATE, S5_GROUP), (2 * S5_GROUP) ** -0.5),
        's5_b_im': nrm((DEPTH, S5_GROUPS, S5_STATE, S5_GROUP), (2 * S5_GROUP) ** -0.5),
        's5_c_re': nrm((DEPTH, S5_GROUPS, S5_GROUP, S5_STATE), (2 * S5_STATE) ** -0.5),
        's5_c_im': nrm((DEPTH, S5_GROUPS, S5_GROUP, S5_STATE), (2 * S5_STATE) ** -0.5),
        's5_d': nrm((DEPTH, BR_WIDTH), 0.5),
        's5_glu_w': nrm((DEPTH, BR_WIDTH, BR_WIDTH), BR_WIDTH ** -0.5),
        's5_glu_b': nrm((DEPTH, BR_WIDTH), 0.02),
        'ml_norm': 1.0 + nrm((DEPTH, ML_HEADS, ML_HEAD_DIM), 0.02),
        'mem_norm': 1.0 + nrm((DEPTH, D_MODEL), 0.02),
        'w_mem_kv': nrm((DEPTH, D_MODEL, 2 * BR_WIDTH), D_MODEL ** -0.5),
        'w_down': nrm((DEPTH, N_BRANCH, BR_WIDTH, D_MODEL), BR_WIDTH ** -0.5),
        'w_out': nrm((DEPTH, D_MODEL, D_MODEL), 0.5 * D_MODEL ** -0.5),
        'final_norm': 1.0 + nrm((D_MODEL,), 0.02),
    }


def reference(x_prompt, x_sample, mem_prompt, cache_mem_k, cache_mem_v, state_ssd_conv, state_ssd,
              state_s5_re, state_s5_im, state_mlstm_c, state_mlstm_n, state_mlstm_m,
              norm_in, w_in, b_gate, b_igate, b_fgate, ssd_conv_w, ssd_conv_b, ssd_dt_bias, ssd_a_log, ssd_d,
              ssd_norm, s5_a_re, s5_a_im, s5_log_dt, s5_b_re, s5_b_im, s5_c_re, s5_c_im, s5_d, s5_glu_w,
              s5_glu_b, ml_norm, mem_norm, w_mem_kv, w_down, w_out, final_norm):
    f32 = jnp.float32

    def zeros(*shape):
        return jnp.zeros((BATCH,) + shape, f32)

    yp, ys = x_prompt, x_sample
    outs_p = [[] for _ in range(9)]
    outs_s = [[] for _ in range(7)]
    for l in range(DEPTH):
        lw = (norm_in[l], w_in[l], b_gate[l], b_igate[l], b_fgate[l], ssd_conv_w[l], ssd_conv_b[l],
              ssd_dt_bias[l], ssd_a_log[l], ssd_d[l], ssd_norm[l], s5_a_re[l], s5_a_im[l], s5_log_dt[l],
              s5_b_re[l], s5_b_im[l], s5_c_re[l], s5_c_im[l], s5_d[l], s5_glu_w[l], s5_glu_b[l],
              ml_norm[l], w_down[l], w_out[l])
        mk, mv = _mem_kv(mem_prompt, mem_norm[l], w_mem_kv[l])
        yp, *st_p = _layer(yp, mk, mv, zeros(SSD_CONV - 1, SSD_CONV_DIM),
                           zeros(SSD_HEADS, SSD_HEAD_DIM, SSD_STATE), zeros(S5_GROUPS, S5_STATE),
                           zeros(S5_GROUPS, S5_STATE), zeros(ML_HEADS, ML_HEAD_DIM, ML_HEAD_DIM),
                           zeros(ML_HEADS, ML_HEAD_DIM), zeros(ML_HEADS), lw)
        ys, *st_s = _layer(ys, cache_mem_k[l], cache_mem_v[l], state_ssd_conv[l], state_ssd[l], state_s5_re[l],
                           state_s5_im[l], state_mlstm_c[l], state_mlstm_n[l], state_mlstm_m[l], lw)
        for lst, a in zip(outs_p, [mk, mv] + st_p):
            lst.append(a)
        for lst, a in zip(outs_s, st_s):
            lst.append(a)
    (mk_p, mv_p, conv_p, ssd_p, s5re_p, s5im_p, mc_p, mn_p, mm_p) = [jnp.stack(a) for a in outs_p]
    (conv_s, ssd_s, s5re_s, s5im_s, mc_s, mn_s, mm_s) = [jnp.stack(a) for a in outs_s]
    y_prompt = _rmsnorm(yp, final_norm)
    y_sample = _rmsnorm(ys, final_norm)
    return (y_prompt, y_sample, mk_p, mv_p, conv_p, ssd_p, s5re_p, s5im_p, mc_p, mn_p, mm_p,
            conv_s, ssd_s, s5re_s, s5im_s, mc_s, mn_s, mm_s)
```

```python
import functools
import math

import jax
import jax.numpy as jnp
from jax import lax
from jax.experimental import pallas as pl
from jax.experimental.pallas import tpu as pltpu

F32 = jnp.float32
BF16 = jnp.bfloat16
NEG_INF = float("-inf")

D_MODEL = 1024
DEPTH = 4
BR_WIDTH = D_MODEL
SSD_HEADS = 16
SSD_HEAD_DIM = 64
SSD_GROUPS = 2
SSD_STATE = 128
SSD_CONV = 4
SSD_CONV_DIM = BR_WIDTH + 2 * SSD_GROUPS * SSD_STATE
S5_GROUP = 16
S5_GROUPS = 64
S5_STATE = 64
S5_CLUSTERS = 4
S5_CLUSTER_GROUPS = S5_GROUPS // S5_CLUSTERS
S5_FLAT = S5_GROUPS * S5_STATE
ML_HEADS = 4
ML_HEAD_DIM = 256
MEM_TOKENS = 256
XA_HEADS = 4
XA_HEAD_DIM = 256
EPS = 1e-6
LANES = 128
CONV_PAD = 8

IN_NAMES = ("z_ssd", "xbc", "dt", "u_s5", "z_s5", "q", "k", "v", "i", "f", "o", "z_ml", "q_xa", "z_xa", "gate")
IN_SIZES = (BR_WIDTH, SSD_CONV_DIM, SSD_HEADS, BR_WIDTH, BR_WIDTH, BR_WIDTH, BR_WIDTH, BR_WIDTH,
            ML_HEADS, ML_HEADS, BR_WIDTH, BR_WIDTH, BR_WIDTH, BR_WIDTH, 4 * D_MODEL)

WA_COLS = SSD_CONV_DIM + BR_WIDTH + LANES
WB_COLS = 2 * BR_WIDTH
WC_COLS = 3 * BR_WIDTH + 2 * LANES + 2 * BR_WIDTH
WD_COLS = 2 * BR_WIDTH

VMEM_LIMIT = 56 * 1024 * 1024


def _dot(a, b):
    return jnp.dot(a, b, preferred_element_type=F32)


def _dot_nt(a, b):
    return lax.dot_general(a, b, (((1,), (1,)), ((), ())), preferred_element_type=F32)


def _split3(x):
    hi = x.astype(BF16)
    r = x - hi.astype(F32)
    mid = r.astype(BF16)
    lo = (r - mid.astype(F32)).astype(BF16)
    return hi, mid, lo


def _sel_left(sel, x):
    hi, mid, lo = _split3(x)
    return _dot(sel, hi) + _dot(sel, mid) + _dot(sel, lo)


def _sel_right(x, sel):
    hi, mid, lo = _split3(x)
    return _dot(hi, sel) + _dot(mid, sel) + _dot(lo, sel)


def _softplus(x):
    return jnp.maximum(x, 0.0) + jnp.log1p(jnp.exp(-jnp.abs(x)))


def _silu(x):
    return x * jax.nn.sigmoid(x)


def _rms(x, g):
    return x * lax.rsqrt(jnp.mean(x * x, axis=-1, keepdims=True) + EPS) * g


def _seq_masks(nb, L):
    M = nb * L
    ri = lax.broadcasted_iota(jnp.int32, (M, M), 0)
    ci = lax.broadcasted_iota(jnp.int32, (M, M), 1)
    if nb > 1:
        sh = int(math.log2(L))
        same = lax.shift_right_logical(ri, sh) == lax.shift_right_logical(ci, sh)
        causal = same & (ci <= ri)
        upper = same & (ri <= ci)
    else:
        same = None
        causal = ci <= ri
        upper = ri <= ci
    return same, causal, upper


def _valid_masks(nb, L, valid):
    M = nb * L
    rv = (lax.broadcasted_iota(jnp.int32, (M, 1), 0) & (L - 1)) < valid
    cv = (lax.broadcasted_iota(jnp.int32, (1, M), 1) & (L - 1)) < valid
    return rv, cv


def _row_sel(nb, L, b, dtype):
    rows = lax.broadcasted_iota(jnp.int32, (nb * L, 1), 0)
    return (lax.shift_right_logical(rows, int(math.log2(L))) == b).astype(dtype)


def _layer_spec(shape, l):
    return pl.BlockSpec((None,) + shape, lambda *_: (l,) + (0,) * len(shape), pipeline_mode=pl.Buffered(1))


def _norm_body(x_ref, g_ref, h_ref, htm_ref):
    hb = _rms(x_ref[0], g_ref[...]).astype(BF16)
    h_ref[0] = hb
    htm_ref[...] = hb


def _norm_call(x, g, tm):
    B, T, _ = x.shape
    return pl.pallas_call(
        _norm_body,
        grid=(B, T // tm),
        in_specs=[pl.BlockSpec((1, tm, D_MODEL), lambda b, t: (b, t, 0)),
                  pl.BlockSpec((1, D_MODEL), lambda b, t: (0, 0))],
        out_specs=[pl.BlockSpec((1, tm, D_MODEL), lambda b, t: (b, t, 0)),
                   pl.BlockSpec((tm, D_MODEL), lambda b, t: (t, b))],
        out_shape=[jax.ShapeDtypeStruct((B, T, D_MODEL), BF16),
                   jax.ShapeDtypeStruct((T, B * D_MODEL), BF16)],
        compiler_params=pltpu.CompilerParams(dimension_semantics=("arbitrary", "arbitrary")),
        name="rmsnorm_in",
    )(x, g)


def _memkv_body(mem_ref, g_ref, w_ref, mk_ref, mv_ref):
    hb = _rms(mem_ref[0], g_ref[0]).astype(BF16)
    kv = _dot(hb, w_ref[0])
    mk_ref[0, 0] = kv[:, :BR_WIDTH]
    mv_ref[0, 0] = kv[:, BR_WIDTH:]


def _memkv_call(mem, g, w_kv):
    B = mem.shape[0]
    out = jax.ShapeDtypeStruct((DEPTH, B, MEM_TOKENS, BR_WIDTH), F32)
    return pl.pallas_call(
        _memkv_body,
        grid=(DEPTH, B),
        in_specs=[pl.BlockSpec((1, MEM_TOKENS, D_MODEL), lambda l, b: (b, 0, 0)),
                  pl.BlockSpec((1, 1, D_MODEL), lambda l, b: (l, 0, 0)),
                  pl.BlockSpec((1, D_MODEL, 2 * BR_WIDTH), lambda l, b: (l, 0, 0))],
        out_specs=[pl.BlockSpec((1, 1, MEM_TOKENS, BR_WIDTH), lambda l, b: (l, b, 0, 0)),
                   pl.BlockSpec((1, 1, MEM_TOKENS, BR_WIDTH), lambda l, b: (l, b, 0, 0))],
        out_shape=[out, out],
        compiler_params=pltpu.CompilerParams(dimension_semantics=("arbitrary", "arbitrary"),
                                             vmem_limit_bytes=VMEM_LIMIT),
        name="mem_kv",
    )(mem, g, w_kv)


def _ssd_body(h_ref, w_ref, wst_ref, prow_ref, pcol_ref, cw_ref, cb_ref, dsk_ref, nrm_ref, buf0_ref, s0_ref,
              y_ref, bufo_ref, so_ref, xp_scr, yin_scr, yst_scr, *, nb, L, valid):
    M = nb * L
    hg = SSD_HEADS // SSD_GROUPS

    @pl.when(pl.program_id(1) == 0)
    def _init():
        so_ref[...] = s0_ref[...]
        xp_scr[:, CONV_PAD - 3:CONV_PAD, :] = buf0_ref[...]

    hb = h_ref[...].reshape(M, D_MODEL)
    proj = _dot(hb, w_ref[...])

    xp_scr[:, CONV_PAD:CONV_PAD + L, :] = proj[:, :SSD_CONV_DIM].reshape(nb, L, SSD_CONV_DIM)
    conv = cb_ref[...][None]
    for k in range(SSD_CONV):
        o = CONV_PAD - 3 + k
        conv = conv + xp_scr[:, o:o + L, :] * cw_ref[k:k + 1, :][None]
    nbuf = xp_scr[:, CONV_PAD - 3 + valid:CONV_PAD + valid, :]
    xp_scr[:, CONV_PAD - 3:CONV_PAD, :] = nbuf
    bufo_ref[...] = nbuf
    xbc = _silu(conv).reshape(M, SSD_CONV_DIM)
    xs = xbc[:, :BR_WIDTH]
    bm = xbc[:, BR_WIDTH:BR_WIDTH + SSD_GROUPS * SSD_STATE].astype(BF16)
    cm = xbc[:, BR_WIDTH + SSD_GROUPS * SSD_STATE:].astype(BF16)
    z = proj[:, SSD_CONV_DIM:SSD_CONV_DIM + BR_WIDTH]

    same, causal, upper = _seq_masks(nb, L)
    tril_b = causal.astype(BF16)
    triu_b = upper.astype(BF16)
    same_b = jnp.ones((M, M), BF16) if same is None else same.astype(BF16)

    dt = _softplus(proj[:, SSD_CONV_DIM + BR_WIDTH:] + prow_ref[0:1, :])
    dtT = _softplus(_dot_nt(wst_ref[...], hb) + pcol_ref[:, 0:1])
    dt = jnp.where(lax.broadcasted_iota(jnp.int32, (1, LANES), 1) < SSD_HEADS, dt, 0.0)
    if valid < L:
        rv, cv = _valid_masks(nb, L, valid)
        dt = jnp.where(rv, dt, 0.0)
        dtT = jnp.where(cv, dtT, 0.0)
    la = dt * (-jnp.exp(prow_ref[1:2, :]))
    laT = dtT * (-jnp.exp(pcol_ref[:, 1:2]))
    acs = _sel_left(tril_b, la)
    tot = _sel_left(same_b, la)
    acsT = _sel_right(laT, triu_b)

    expand = (lax.shift_right_logical(lax.broadcasted_iota(jnp.int32, (LANES, BR_WIDTH), 1), 6)
              == lax.broadcasted_iota(jnp.int32, (LANES, BR_WIDTH), 0)).astype(BF16)
    dt_x = _sel_right(dt, expand)
    tail_x = _sel_right(jnp.exp(tot - acs), expand)
    eacs_x = _sel_right(jnp.exp(acs), expand)
    eend = jnp.exp(tot)

    xdt = xs * dt_x
    xdt_b = xdt.astype(BF16)
    at = (xdt * tail_x).T.astype(BF16)

    for g in range(SSD_GROUPS):
        cm_g = cm[:, g * SSD_STATE:(g + 1) * SSD_STATE]
        bm_g = bm[:, g * SSD_STATE:(g + 1) * SSD_STATE]
        cb = _dot_nt(cm_g, bm_g)
        for hh in range(hg):
            h = g * hg + hh
            lo, hi = h * SSD_HEAD_DIM, (h + 1) * SSD_HEAD_DIM
            seg = acs[:, h:h + 1] - acsT[h:h + 1, :]
            decay = jnp.exp(jnp.where(causal, seg, NEG_INF))
            yin_scr[:, lo:hi] = _dot((cb * decay).astype(BF16), xdt_b[:, lo:hi])
        for b in range(nb):
            r0, r1 = b * L, (b + 1) * L
            bm_b = bm_g if nb == 1 else bm_g * _row_sel(nb, L, b, BF16)
            for hh in range(hg):
                h = g * hg + hh
                lo, hi = h * SSD_HEAD_DIM, (h + 1) * SSD_HEAD_DIM
                s_old = so_ref[b, h]
                yst_scr[r0:r1, lo:hi] = _dot_nt(cm_g[r0:r1, :], s_old.astype(BF16))
                so_ref[b, h] = eend[r0:r0 + 1, h:h + 1] * s_old + _dot(at[lo:hi, :], bm_b)

    y = yin_scr[...] + eacs_x * yst_scr[...] + dsk_ref[...] * xs
    y_ref[...] = _rms(y * _silu(z), nrm_ref[...]).reshape(nb, L, BR_WIDTH).astype(BF16)


def _ssd_call(h, wA, wstA, prowA, pcolA, conv_w, conv_b, dskA, nrmA, buf0, s0, l, *, nb, L, valid):
    B, T, _ = h.shape
    grid = (B // nb, T // L)
    wspec = lambda shape: _layer_spec(shape, l)
    body = functools.partial(_ssd_body, nb=nb, L=L, valid=valid)
    return pl.pallas_call(
        body,
        grid=grid,
        in_specs=[pl.BlockSpec((nb, L, D_MODEL), lambda i, c: (i, c, 0)),
                  wspec((D_MODEL, WA_COLS)), wspec((SSD_HEADS, D_MODEL)),
                  wspec((8, LANES)), wspec((SSD_HEADS, 2)),
                  wspec((SSD_CONV, SSD_CONV_DIM)), wspec((1, SSD_CONV_DIM)),
                  wspec((1, BR_WIDTH)), wspec((1, BR_WIDTH)),
                  pl.BlockSpec((nb, SSD_CONV - 1, SSD_CONV_DIM), lambda i, c: (i, 0, 0)),
                  pl.BlockSpec((nb, SSD_HEADS, SSD_HEAD_DIM, SSD_STATE), lambda i, c: (i, 0, 0, 0))],
        out_specs=[pl.BlockSpec((nb, L, BR_WIDTH), lambda i, c: (i, c, 0)),
                   pl.BlockSpec((nb, SSD_CONV - 1, SSD_CONV_DIM), lambda i, c: (i, 0, 0)),
                   pl.BlockSpec((nb, SSD_HEADS, SSD_HEAD_DIM, SSD_STATE), lambda i, c: (i, 0, 0, 0))],
        out_shape=[jax.ShapeDtypeStruct((B, T, BR_WIDTH), BF16),
                   jax.ShapeDtypeStruct((B, SSD_CONV - 1, SSD_CONV_DIM), F32),
                   jax.ShapeDtypeStruct((B, SSD_HEADS, SSD_HEAD_DIM, SSD_STATE), F32)],
        scratch_shapes=[pltpu.VMEM((nb, L + CONV_PAD, SSD_CONV_DIM), F32),
                        pltpu.VMEM((nb * L, BR_WIDTH), F32),
                        pltpu.VMEM((nb * L, BR_WIDTH), F32)],
        compiler_params=pltpu.CompilerParams(dimension_semantics=("arbitrary", "arbitrary"),
                                             vmem_limit_bytes=VMEM_LIMIT),
        name="ssd_branch",
    )(h, wA, wstA, prowA, pcolA, conv_w, conv_b, dskA, nrmA, buf0, s0)


S5_SCAN_LANES = 512


def _s5_body(h_ref, w_ref, bc_ref, cc_ref, lam_ref, dsk_ref, gw_ref, gb_ref, sre0_ref, sim0_ref,
             y_ref, sre_ref, sim_ref, hs_scr, *, nb, Lt):
    half = S5_CLUSTER_GROUPS * S5_STATE

    @pl.when(pl.program_id(1) == 0)
    def _init():
        sre_ref[...] = sre0_ref[...]
        sim_ref[...] = sim0_ref[...]

    hb = h_ref[0]
    proj = _dot(hb, w_ref[...])
    u = proj[:, :BR_WIDTH]
    z = proj[:, BR_WIDTH:]
    ub = u.astype(BF16)
    gl = S5_CLUSTER_GROUPS * S5_GROUP
    for j in range(S5_CLUSTERS):
        hs_scr[:, 2 * half * j:2 * half * (j + 1)] = _dot(ub[:, gl * j:gl * (j + 1)], bc_ref[j])

    for j in range(S5_CLUSTERS):
        for q in range(half // S5_SCAN_LANES):
            cr = 2 * half * j + S5_SCAN_LANES * q
            ci = cr + half
            sc = half * j + S5_SCAN_LANES * q
            lr = jnp.broadcast_to(lam_ref[0:1, sc:sc + S5_SCAN_LANES], (8, S5_SCAN_LANES))
            li = jnp.broadcast_to(lam_ref[1:2, sc:sc + S5_SCAN_LANES], (8, S5_SCAN_LANES))
            for sg in range(nb // 8):
                def step(t, carry, cr=cr, ci=ci, lr=lr, li=li, sg=sg):
                    hr, hi = carry
                    r0 = pl.multiple_of(t * nb + 8 * sg, 8)
                    nr = lr * hr - li * hi + hs_scr[pl.ds(r0, 8), cr:cr + S5_SCAN_LANES]
                    ni = lr * hi + li * hr + hs_scr[pl.ds(r0, 8), ci:ci + S5_SCAN_LANES]
                    hs_scr[pl.ds(r0, 8), cr:cr + S5_SCAN_LANES] = nr
                    hs_scr[pl.ds(r0, 8), ci:ci + S5_SCAN_LANES] = ni
                    return nr, ni
                init = (sre_ref[0, 8 * sg:8 * sg + 8, sc:sc + S5_SCAN_LANES],
                        sim_ref[0, 8 * sg:8 * sg + 8, sc:sc + S5_SCAN_LANES])
                hr, hi = lax.fori_loop(0, Lt, step, init, unroll=min(Lt, 4))
                sre_ref[0, 8 * sg:8 * sg + 8, sc:sc + S5_SCAN_LANES] = hr
                sim_ref[0, 8 * sg:8 * sg + 8, sc:sc + S5_SCAN_LANES] = hi

    ys = [_dot(hs_scr[:, 2 * half * j:2 * half * (j + 1)].astype(BF16), cc_ref[j]) for j in range(S5_CLUSTERS)]
    y = jnp.concatenate(ys, axis=1) + dsk_ref[...] * u
    yb = jax.nn.gelu(y)
    glu = jax.nn.sigmoid(_dot(yb.astype(BF16), gw_ref[...]) + gb_ref[...])
    y_ref[0] = (yb * glu * _silu(z)).astype(BF16)


def _s5_call(h_tm, wB, bc, cc, lam, dskB, glu_w, glu_b, sre0, sim0, l, *, nb, Lt):
    NB, R, _ = h_tm.shape
    T = R // nb
    grid = (NB, T // Lt)
    wspec = lambda shape: _layer_spec(shape, l)
    st_spec = pl.BlockSpec((1, nb, S5_FLAT), lambda i, c: (i, 0, 0))
    body = functools.partial(_s5_body, nb=nb, Lt=Lt)
    return pl.pallas_call(
        body,
        grid=grid,
        in_specs=[pl.BlockSpec((1, nb * Lt, D_MODEL), lambda i, c: (i, c, 0)),
                  wspec((D_MODEL, WB_COLS)),
                  wspec((S5_CLUSTERS, S5_CLUSTER_GROUPS * S5_GROUP, 2 * S5_CLUSTER_GROUPS * S5_STATE)),
                  wspec((S5_CLUSTERS, 2 * S5_CLUSTER_GROUPS * S5_STATE, S5_CLUSTER_GROUPS * S5_GROUP)),
                  wspec((2, S5_FLAT)), wspec((1, BR_WIDTH)),
                  wspec((BR_WIDTH, BR_WIDTH)), wspec((1, BR_WIDTH)),
                  st_spec, st_spec],
        out_specs=[pl.BlockSpec((1, nb * Lt, BR_WIDTH), lambda i, c: (i, c, 0)), st_spec, st_spec],
        out_shape=[jax.ShapeDtypeStruct((NB, R, BR_WIDTH), BF16),
                   jax.ShapeDtypeStruct((NB, nb, S5_FLAT), F32),
                   jax.ShapeDtypeStruct((NB, nb, S5_FLAT), F32)],
        scratch_shapes=[pltpu.VMEM((nb * Lt, 2 * S5_FLAT), F32)],
        compiler_params=pltpu.CompilerParams(dimension_semantics=("arbitrary", "arbitrary"),
                                             vmem_limit_bytes=VMEM_LIMIT),
        name="s5_branch",
    )(h_tm, wB, bc, cc, lam, dskB, glu_w, glu_b, sre0, sim0)


def _mlstm_body(h_ref, w_ref, wst_ref, prow_ref, pcol_ref, nrm_ref, c0_ref, n0_ref, m0_ref,
                y_ref, co_ref, no_ref, mo_ref, num_scr, *, nb, L, valid):
    M = nb * L
    W = BR_WIDTH

    @pl.when(pl.program_id(1) == 0)
    def _init():
        co_ref[...] = c0_ref[...]
        no_ref[...] = n0_ref[...]
        mo_ref[...] = m0_ref[...]

    hb = h_ref[...].reshape(M, D_MODEL)
    proj = _dot(hb, w_ref[...])
    q = proj[:, :W]
    k = proj[:, W:2 * W] * (ML_HEAD_DIM ** -0.5)
    v = proj[:, 2 * W:3 * W]
    ig = proj[:, 3 * W:3 * W + LANES] + prow_ref[0:1, :]
    logf = -_softplus(-(proj[:, 3 * W + LANES:3 * W + 2 * LANES] + prow_ref[1:2, :]))
    o = proj[:, 3 * W + 2 * LANES:4 * W + 2 * LANES]
    z = proj[:, 4 * W + 2 * LANES:]
    smallT = _dot_nt(wst_ref[...], hb) + pcol_ref[...]
    igT = smallT[0:8, :]
    logfT = -_softplus(-smallT[8:16, :])
    if valid < L:
        rv, cv = _valid_masks(nb, L, valid)
        ig = jnp.where(rv, ig, NEG_INF)
        logf = jnp.where(rv, logf, 0.0)
        igT = jnp.where(cv, igT, NEG_INF)
        logfT = jnp.where(cv, logfT, 0.0)

    same, causal, upper = _seq_masks(nb, L)
    tril_b = causal.astype(BF16)
    triu_b = upper.astype(BF16)
    same_b = jnp.ones((M, M), BF16) if same is None else same.astype(BF16)
    bcum = _sel_left(tril_b, logf)
    tot = _sel_left(same_b, logf)
    bcumT = _sel_right(logfT, triu_b)
    totT = _sel_right(logfT, same_b)
    m_rows = jnp.broadcast_to(mo_ref[...], (nb, L, LANES)).reshape(M, LANES)
    gcar = bcum + m_rows
    g_end = tot + m_rows
    d_end = tot - bcum + ig
    d_endT = totT - bcumT + igT

    qb = q.astype(BF16)
    kb = k.astype(BF16)
    vb = v.astype(BF16)
    lane = lax.broadcasted_iota(jnp.int32, (M, LANES), 1)
    m_new_all = jnp.zeros((M, LANES), F32)
    per_head = []
    wk_parts = []
    for hd in range(ML_HEADS):
        lo, hi = hd * ML_HEAD_DIM, (hd + 1) * ML_HEAD_DIM
        dmat = jnp.where(causal, bcum[:, hd:hd + 1] - bcumT[hd:hd + 1, :] + igT[hd:hd + 1, :], NEG_INF)
        m_l = jnp.maximum(gcar[:, hd:hd + 1], jnp.max(dmat, axis=1, keepdims=True))
        w_inter = jnp.exp(gcar[:, hd:hd + 1] - m_l)
        qk = _dot_nt(qb[:, lo:hi], kb[:, lo:hi]) * jnp.exp(dmat - m_l)
        num_scr[:, lo:hi] = _dot(qk.astype(BF16), vb[:, lo:hi])
        den_intra = jnp.sum(qk, axis=1, keepdims=True)
        d_row = d_endT[hd:hd + 1, :] if same is None else jnp.where(same, d_endT[hd:hd + 1, :], NEG_INF)
        m_new = jnp.maximum(g_end[:, hd:hd + 1], jnp.max(d_row, axis=1, keepdims=True))
        w_end = jnp.exp(d_end[:, hd:hd + 1] - m_new)
        dec = jnp.exp(g_end[:, hd:hd + 1] - m_new)
        wk_parts.append(w_end * k[:, lo:hi])
        m_new_all = jnp.where(lane == hd, m_new, m_new_all)
        per_head.append((m_l, w_inter, den_intra, dec))
    wk = jnp.concatenate(wk_parts, axis=1)
    kt = wk.T.astype(BF16)

    for b in range(nb):
        r0, r1 = b * L, (b + 1) * L
        sel = None if nb == 1 else _row_sel(nb, L, b, BF16)
        for hd in range(ML_HEADS):
            lo, hi = hd * ML_HEAD_DIM, (hd + 1) * ML_HEAD_DIM
            m_l, w_inter, den_intra, dec = per_head[hd]
            c_old = co_ref[b, hd]
            n_old = no_ref[b, hd:hd + 1, :]
            q_r = q[r0:r1, lo:hi]
            wi = w_inter[r0:r1]
            num = num_scr[r0:r1, lo:hi] + wi * _dot(qb[r0:r1, lo:hi], c_old.astype(BF16))
            den = den_intra[r0:r1] + wi * jnp.sum(q_r * n_old, axis=1, keepdims=True)
            hout = num / jnp.maximum(jnp.abs(den), jnp.exp(-m_l[r0:r1]))
            ho = jax.nn.sigmoid(o[r0:r1, lo:hi]) * hout
            yv = _rms(ho, nrm_ref[:, lo:hi]) * _silu(z[r0:r1, lo:hi])
            y_ref[b, :, lo:hi] = yv.astype(BF16)
            dec_b = dec[r0:r0 + 1]
            vb_b = vb[:, lo:hi] if sel is None else vb[:, lo:hi] * sel
            co_ref[b, hd] = dec_b * c_old + _dot(kt[lo:hi, :], vb_b)
            no_ref[b, hd:hd + 1, :] = dec_b * n_old + jnp.sum(wk[r0:r1, lo:hi], axis=0, keepdims=True)
        mo_ref[b] = m_new_all[r0:r0 + 1, :]


def _mlstm_call(h, wC, wstC, prowC, pcolC, nrmC, c0, n0, m0, l, *, nb, L, valid):
    B, T, _ = h.shape
    grid = (B // nb, T // L)
    wspec = lambda shape: _layer_spec(shape, l)
    c_spec = pl.BlockSpec((nb, ML_HEADS, ML_HEAD_DIM, ML_HEAD_DIM), lambda i, c: (i, 0, 0, 0))
    n_spec = pl.BlockSpec((nb, ML_HEADS, ML_HEAD_DIM), lambda i, c: (i, 0, 0))
    m_spec = pl.BlockSpec((nb, 1, LANES), lambda i, c: (i, 0, 0))
    body = functools.partial(_mlstm_body, nb=nb, L=L, valid=valid)
    return pl.pallas_call(
        body,
        grid=grid,
        in_specs=[pl.BlockSpec((nb, L, D_MODEL), lambda i, c: (i, c, 0)),
                  wspec((D_MODEL, WC_COLS)), wspec((16, D_MODEL)),
                  wspec((8, LANES)), wspec((16, 1)), wspec((1, BR_WIDTH)),
                  c_spec, n_spec, m_spec],
        out_specs=[pl.BlockSpec((nb, L, BR_WIDTH), lambda i, c: (i, c, 0)), c_spec, n_spec, m_spec],
        out_shape=[jax.ShapeDtypeStruct((B, T, BR_WIDTH), BF16),
                   jax.ShapeDtypeStruct((B, ML_HEADS, ML_HEAD_DIM, ML_HEAD_DIM), F32),
                   jax.ShapeDtypeStruct((B, ML_HEADS, ML_HEAD_DIM), F32),
                   jax.ShapeDtypeStruct((B, 1, LANES), F32)],
        scratch_shapes=[pltpu.VMEM((nb * L, BR_WIDTH), F32)],
        compiler_params=pltpu.CompilerParams(dimension_semantics=("arbitrary", "arbitrary"),
                                             vmem_limit_bytes=VMEM_LIMIT),
        name="mlstm_branch",
    )(h, wC, wstC, prowC, pcolC, nrmC, c0, n0, m0)


def _xattn_body(h_ref, w_ref, mk_ref, mv_ref, y_ref, *, nb, L):
    M = nb * L
    hb = h_ref[...].reshape(M, D_MODEL)
    proj = _dot(hb, w_ref[...])
    qb = proj[:, :BR_WIDTH].astype(BF16)
    z = proj[:, BR_WIDTH:]
    for b in range(nb):
        r0, r1 = b * L, (b + 1) * L
        for hd in range(XA_HEADS):
            lo, hi = hd * XA_HEAD_DIM, (hd + 1) * XA_HEAD_DIM
            s = _dot_nt(qb[r0:r1, lo:hi], mk_ref[b, :, lo:hi].astype(BF16)) * (XA_HEAD_DIM ** -0.5)
            e = jnp.exp(s - jnp.max(s, axis=1, keepdims=True))
            p = e / jnp.sum(e, axis=1, keepdims=True)
            a = _dot(p.astype(BF16), mv_ref[b, :, lo:hi].astype(BF16))
            y_ref[b, :, lo:hi] = (a * _silu(z[r0:r1, lo:hi])).astype(BF16)


def _xattn_call(h, wD, mk, mv, l, lkv, *, nb, L):
    B, T, _ = h.shape
    grid = (B // nb, T // L)
    kv_spec = pl.BlockSpec((None, nb, MEM_TOKENS, BR_WIDTH), lambda i, c: (lkv, i, 0, 0))
    body = functools.partial(_xattn_body, nb=nb, L=L)
    return pl.pallas_call(
        body,
        grid=grid,
        in_specs=[pl.BlockSpec((nb, L, D_MODEL), lambda i, c: (i, c, 0)),
                  _layer_spec((D_MODEL, WD_COLS), l),
                  kv_spec, kv_spec],
        out_specs=pl.BlockSpec((nb, L, BR_WIDTH), lambda i, c: (i, c, 0)),
        out_shape=jax.ShapeDtypeStruct((B, T, BR_WIDTH), BF16),
        compiler_params=pltpu.CompilerParams(dimension_semantics=("arbitrary", "arbitrary"),
                                             vmem_limit_bytes=VMEM_LIMIT),
        name="xattn_branch",
    )(h, wD, mk, mv)


def _merge_body(h_ref, ya_ref, yb_ref, yc_ref, yd_ref, x_ref, wg_ref, bg_ref, wd_ref, wo_ref, gn_ref, *outs, last):
    h = h_ref[0]
    ys = (ya_ref[0], yb_ref[...], yc_ref[0], yd_ref[0])
    merged = None
    for kbr in range(4):
        lo, hi = kbr * D_MODEL, (kbr + 1) * D_MODEL
        gate = jax.nn.sigmoid(_dot(h, wg_ref[:, lo:hi]) + bg_ref[:, lo:hi])
        term = gate * _dot(ys[kbr], wd_ref[kbr])
        merged = term if merged is None else merged + term
    xn = x_ref[0] + _dot(merged.astype(BF16), wo_ref[...])
    if last:
        outs[0][0] = _rms(xn, gn_ref[...])
    else:
        outs[0][0] = xn
        hb = _rms(xn, gn_ref[...]).astype(BF16)
        outs[1][0] = hb
        outs[2][...] = hb


def _merge_call(h, ya, yb_tm, yc, yd, x, wE, b_gate, w_down, w_out, g_next, l, *, tm, last):
    B, T, _ = x.shape
    row = pl.BlockSpec((1, tm, D_MODEL), lambda b, t: (b, t, 0))
    tmaj = pl.BlockSpec((tm, D_MODEL), lambda b, t: (t, b))
    wspec = lambda shape: _layer_spec(shape, l)
    if last:
        out_specs = [row]
        out_shape = [jax.ShapeDtypeStruct((B, T, D_MODEL), F32)]
    else:
        out_specs = [row, row, tmaj]
        out_shape = [jax.ShapeDtypeStruct((B, T, D_MODEL), F32),
                     jax.ShapeDtypeStruct((B, T, D_MODEL), BF16),
                     jax.ShapeDtypeStruct((T, B * D_MODEL), BF16)]
    return pl.pallas_call(
        functools.partial(_merge_body, last=last),
        grid=(B, T // tm),
        in_specs=[row, row, tmaj, row, row, row,
                  wspec((D_MODEL, 4 * D_MODEL)), wspec((1, 4 * D_MODEL)),
                  wspec((4, BR_WIDTH, D_MODEL)), wspec((D_MODEL, D_MODEL)),
                  pl.BlockSpec((1, D_MODEL), lambda b, t: (0, 0))],
        out_specs=out_specs,
        out_shape=out_shape,
        compiler_params=pltpu.CompilerParams(dimension_semantics=("arbitrary", "arbitrary"),
                                             vmem_limit_bytes=VMEM_LIMIT),
        name="merge_out",
    )(h, ya, yb_tm, yc, yd, x, wE, b_gate, w_down, w_out, g_next)


def _pack_params(w_in, b_gate, b_igate, b_fgate, ssd_conv_w, ssd_conv_b, ssd_dt_bias, ssd_a_log, ssd_d, ssd_norm,
                 s5_a_re, s5_a_im, s5_log_dt, s5_b_re, s5_b_im, s5_c_re, s5_c_im, s5_d, s5_glu_w, s5_glu_b,
                 ml_norm, w_down, w_out):
    offs = {}
    acc = 0
    for name, size in zip(IN_NAMES, IN_SIZES):
        offs[name] = (acc, acc + size)
        acc += size
    col = lambda name: w_in[:, :, offs[name][0]:offs[name][1]]
    padl = lambda a: jnp.pad(a, ((0, 0), (0, 0), (0, LANES - a.shape[-1])))
    p = {}
    p["wA"] = jnp.concatenate([col("xbc"), col("z_ssd"), padl(col("dt"))], axis=-1).astype(BF16)
    p["wstA"] = jnp.swapaxes(col("dt"), 1, 2).astype(BF16)
    p["wB"] = jnp.concatenate([col("u_s5"), col("z_s5")], axis=-1).astype(BF16)
    p["wC"] = jnp.concatenate([col("q"), col("k"), col("v"), padl(col("i")), padl(col("f")), col("o"),
                               col("z_ml")], axis=-1).astype(BF16)
    zrow = jnp.zeros((DEPTH, 4, D_MODEL), F32)
    p["wstC"] = jnp.concatenate([jnp.swapaxes(col("i"), 1, 2), zrow, jnp.swapaxes(col("f"), 1, 2), zrow],
                                axis=1).astype(BF16)
    p["wD"] = jnp.concatenate([col("q_xa"), col("z_xa")], axis=-1).astype(BF16)
    p["wE"] = col("gate").astype(BF16)
    p["b_gate"] = b_gate[:, None, :]
    p["w_down"] = w_down.astype(BF16)
    p["w_out"] = w_out.astype(BF16)

    padv = lambda a: jnp.pad(a, ((0, 0), (0, LANES - a.shape[-1])))
    zl = jnp.zeros((DEPTH, 6, LANES), F32)
    p["prowA"] = jnp.concatenate([padv(ssd_dt_bias)[:, None], padv(ssd_a_log)[:, None], zl], axis=1)
    p["pcolA"] = jnp.stack([ssd_dt_bias, ssd_a_log], axis=-1)
    p["conv_w"] = ssd_conv_w
    p["conv_b"] = ssd_conv_b[:, None, :]
    p["dskA"] = jnp.repeat(ssd_d, SSD_HEAD_DIM, axis=-1)[:, None, :]
    p["nrmA"] = ssd_norm[:, None, :]

    p["prowC"] = jnp.concatenate([padv(b_igate)[:, None], padv(b_fgate)[:, None], zl], axis=1)
    z4 = jnp.zeros((DEPTH, 4), F32)
    p["pcolC"] = jnp.concatenate([b_igate, z4, b_fgate, z4], axis=1)[:, :, None]
    p["nrmC"] = ml_norm.reshape(DEPTH, 1, BR_WIDTH)

    dt = jnp.exp(s5_log_dt)[:, :, None]
    mag = jnp.exp(s5_a_re * dt)
    lr = mag * jnp.cos(s5_a_im * dt)
    li = mag * jnp.sin(s5_a_im * dt)
    den = s5_a_re * s5_a_re + s5_a_im * s5_a_im
    cr = ((lr - 1.0) * s5_a_re + li * s5_a_im) / den
    ci = (li * s5_a_re - (lr - 1.0) * s5_a_im) / den
    bb_re = cr[..., None] * s5_b_re - ci[..., None] * s5_b_im
    bb_im = cr[..., None] * s5_b_im + ci[..., None] * s5_b_re
    eye = jnp.eye(S5_CLUSTER_GROUPS, dtype=F32)
    cg = S5_CLUSTER_GROUPS

    def pack_b(bb):
        bb = bb.reshape(DEPTH, S5_CLUSTERS, cg, S5_STATE, S5_GROUP)
        return jnp.einsum("ljgnc,gh->ljgchn", bb, eye).reshape(DEPTH, S5_CLUSTERS, cg * S5_GROUP, cg * S5_STATE)

    def pack_c(cc):
        cc = cc.reshape(DEPTH, S5_CLUSTERS, cg, S5_GROUP, S5_STATE)
        return jnp.einsum("ljgcn,gh->ljgnhc", cc, eye).reshape(DEPTH, S5_CLUSTERS, cg * S5_STATE, cg * S5_GROUP)

    p["bc"] = jnp.concatenate([pack_b(bb_re), pack_b(bb_im)], axis=-1).astype(BF16)
    p["cc"] = jnp.concatenate([pack_c(s5_c_re), pack_c(-s5_c_im)], axis=-2).astype(BF16)
    p["lam"] = jnp.stack([lr.reshape(DEPTH, S5_FLAT), li.reshape(DEPTH, S5_FLAT)], axis=1)
    p["dskB"] = s5_d[:, None, :]
    p["glu_w"] = s5_glu_w.astype(BF16)
    p["glu_b"] = s5_glu_b[:, None, :]
    return p


def _group_cfg(B, T):
    if T % 256 == 0:
        return dict(Tp=None, ssd=(1, 256, 256), ml=(1, 256, 256), xa=(1, 256), s5=(8, 32), tm=256)
    assert T == 4 and B % 32 == 0
    return dict(Tp=True, ssd=(8, 16, T), ml=(4, 32, T), xa=(4, 8), s5=(32, T), tm=B * T)


def _pad_time(h, Lp):
    return jnp.pad(h, ((0, 0), (0, Lp - h.shape[1]), (0, 0)))


def _layer(p, l, x, h, h_tm, mk, mv, lkv, conv0, ssd0, sre0, sim0, c0, n0, m0, g_next, last):
    B, T, _ = x.shape
    cfg = _group_cfg(B, T)
    padded = cfg["Tp"] is not None

    nb, L, valid = cfg["ssd"]
    hin = _pad_time(h, L) if padded else h
    ya, conv_n, ssd_n = _ssd_call(hin, p["wA"], p["wstA"], p["prowA"], p["pcolA"], p["conv_w"], p["conv_b"],
                                  p["dskA"], p["nrmA"], conv0, ssd0, l, nb=nb, L=L, valid=valid)
    nb, L, valid = cfg["ml"]
    hin = _pad_time(h, L) if padded else h
    m0p = jnp.pad(m0, ((0, 0), (0, LANES - ML_HEADS)))[:, None, :]
    yc, c_n, n_n, m_n = _mlstm_call(hin, p["wC"], p["wstC"], p["prowC"], p["pcolC"], p["nrmC"], c0, n0, m0p, l,
                                    nb=nb, L=L, valid=valid)
    m_n = m_n[:, 0, :ML_HEADS]
    nb, L = cfg["xa"]
    hin = _pad_time(h, L) if padded else h
    yd = _xattn_call(hin, p["wD"], mk, mv, l, lkv, nb=nb, L=L)
    nb, Lt = cfg["s5"]
    NB = B // nb
    yb, sre_n, sim_n = _s5_call(h_tm, p["wB"], p["bc"], p["cc"], p["lam"], p["dskB"], p["glu_w"], p["glu_b"],
                                sre0.reshape(NB, nb, S5_FLAT), sim0.reshape(NB, nb, S5_FLAT), l, nb=nb, Lt=Lt)
    sre_n = sre_n.reshape(B, S5_GROUPS, S5_STATE)
    sim_n = sim_n.reshape(B, S5_GROUPS, S5_STATE)

    if padded:
        flat = lambda a: a[:, :T].reshape(1, B * T, D_MODEL)
        yb_flat = yb.reshape(NB, T, nb, D_MODEL).transpose(0, 2, 1, 3).reshape(B * T, D_MODEL)
        outs = _merge_call(h.reshape(1, B * T, D_MODEL), flat(ya), yb_flat, flat(yc), flat(yd),
                           x.reshape(1, B * T, D_MODEL), p["wE"], p["b_gate"], p["w_down"], p["w_out"], g_next, l,
                           tm=cfg["tm"], last=last)
        if last:
            res = (outs[0].reshape(B, T, D_MODEL), None, None)
        else:
            hn = outs[1].reshape(B, T, D_MODEL)
            hn_tm = hn.reshape(NB, nb, T, D_MODEL).transpose(0, 2, 1, 3).reshape(NB, T * nb, D_MODEL)
            res = (outs[0].reshape(B, T, D_MODEL), hn, hn_tm)
    else:
        outs = _merge_call(h, ya, yb.reshape(T, B * D_MODEL), yc, yd, x, p["wE"], p["b_gate"], p["w_down"],
                           p["w_out"], g_next, l, tm=cfg["tm"], last=last)
        if last:
            res = (outs[0], None, None)
        else:
            res = (outs[0], outs[1], outs[2].reshape(1, T * B, D_MODEL))
    return res, (conv_n, ssd_n, sre_n, sim_n, c_n, n_n, m_n)


def _first_norm(x, g):
    B, T, _ = x.shape
    cfg = _group_cfg(B, T)
    if cfg["Tp"] is None:
        h, h_tm = _norm_call(x, g, cfg["tm"])
        return h, h_tm.reshape(1, T * B, D_MODEL)
    h, _ = _norm_call(x.reshape(1, B * T, D_MODEL), g, B * T)
    h = h.reshape(B, T, D_MODEL)
    nb = cfg["s5"][0]
    h_tm = h.reshape(B // nb, nb, T, D_MODEL).transpose(0, 2, 1, 3).reshape(B // nb, T * nb, D_MODEL)
    return h, h_tm


def kernel(x_prompt, x_sample, mem_prompt, cache_mem_k, cache_mem_v, state_ssd_conv, state_ssd, state_s5_re,
           state_s5_im, state_mlstm_c, state_mlstm_n, state_mlstm_m, norm_in, w_in, b_gate, b_igate, b_fgate,
           ssd_conv_w, ssd_conv_b, ssd_dt_bias, ssd_a_log, ssd_d, ssd_norm, s5_a_re, s5_a_im, s5_log_dt, s5_b_re,
           s5_b_im, s5_c_re, s5_c_im, s5_d, s5_glu_w, s5_glu_b, ml_norm, mem_norm, w_mem_kv, w_down, w_out,
           final_norm):
    p = _pack_params(w_in, b_gate, b_igate, b_fgate, ssd_conv_w, ssd_conv_b, ssd_dt_bias, ssd_a_log, ssd_d,
                     ssd_norm, s5_a_re, s5_a_im, s5_log_dt, s5_b_re, s5_b_im, s5_c_re, s5_c_im, s5_d, s5_glu_w,
                     s5_glu_b, ml_norm, w_down, w_out)
    Bp, Tp, _ = x_prompt.shape
    Bs, Ts, _ = x_sample.shape

    mk_p, mv_p = _memkv_call(mem_prompt, mem_norm[:, None, :], w_mem_kv.astype(BF16))
    cache_k = cache_mem_k.reshape(DEPTH, Bs, MEM_TOKENS, BR_WIDTH)
    cache_v = cache_mem_v.reshape(DEPTH, Bs, MEM_TOKENS, BR_WIDTH)

    zeros = lambda *s: jnp.zeros((Bp,) + s, F32)
    zero_states = (zeros(SSD_CONV - 1, SSD_CONV_DIM), zeros(SSD_HEADS, SSD_HEAD_DIM, SSD_STATE),
                   zeros(S5_GROUPS, S5_STATE), zeros(S5_GROUPS, S5_STATE),
                   zeros(ML_HEADS, ML_HEAD_DIM, ML_HEAD_DIM), zeros(ML_HEADS, ML_HEAD_DIM), zeros(ML_HEADS))

    g0 = norm_in[0][None, :]
    xp, (hp, hp_tm) = x_prompt, _first_norm(x_prompt, g0)
    xs, (hs, hs_tm) = x_sample, _first_norm(x_sample, g0)
    st_p, st_s = [], []
    for l in range(DEPTH):
        last = l == DEPTH - 1
        g_next = final_norm[None, :] if last else norm_in[l + 1][None, :]
        (xp, hp, hp_tm), sp = _layer(p, l, xp, hp, hp_tm, mk_p, mv_p, l, *zero_states, g_next, last)
        (xs, hs, hs_tm), ss = _layer(p, l, xs, hs, hs_tm, cache_k, cache_v, l, state_ssd_conv[l], state_ssd[l],
                                     state_s5_re[l], state_s5_im[l], state_mlstm_c[l], state_mlstm_n[l],
                                     state_mlstm_m[l], g_next, last)
        st_p.append(sp)
        st_s.append(ss)
    stack = lambda sts, i: jnp.stack([s[i] for s in sts])
    mk_out = mk_p.reshape(DEPTH, Bp, MEM_TOKENS, XA_HEADS, XA_HEAD_DIM)
    mv_out = mv_p.reshape(DEPTH, Bp, MEM_TOKENS, XA_HEADS, XA_HEAD_DIM)
    return ((xp, xs, mk_out, mv_out) + tuple(stack(st_p, i) for i in range(7))
            + tuple(stack(st_s, i) for i in range(7)))
```

```python
import functools
import math

import jax
import jax.numpy as jnp
from jax import lax
from jax.experimental import pallas as pl
from jax.experimental.pallas import tpu as pltpu

F32 = jnp.float32
BF16 = jnp.bfloat16
NEG_INF = float("-inf")

D_MODEL = 1024
DEPTH = 4
BR_WIDTH = D_MODEL
SSD_HEADS = 16
SSD_HEAD_DIM = 64
SSD_GROUPS = 2
SSD_STATE = 128
SSD_CONV = 4
SSD_CONV_DIM = BR_WIDTH + 2 * SSD_GROUPS * SSD_STATE
S5_GROUP = 16
S5_GROUPS = 64
S5_STATE = 64
S5_CLUSTERS = 4
S5_CLUSTER_GROUPS = S5_GROUPS // S5_CLUSTERS
S5_FLAT = S5_GROUPS * S5_STATE
ML_HEADS = 4
ML_HEAD_DIM = 256
MEM_TOKENS = 256
XA_HEADS = 4
XA_HEAD_DIM = 256
EPS = 1e-6
LANES = 128
CONV_PAD = 8

IN_NAMES = ("z_ssd", "xbc", "dt", "u_s5", "z_s5", "q", "k", "v", "i", "f", "o", "z_ml", "q_xa", "z_xa", "gate")
IN_SIZES = (BR_WIDTH, SSD_CONV_DIM, SSD_HEADS, BR_WIDTH, BR_WIDTH, BR_WIDTH, BR_WIDTH, BR_WIDTH,
            ML_HEADS, ML_HEADS, BR_WIDTH, BR_WIDTH, BR_WIDTH, BR_WIDTH, 4 * D_MODEL)

WA_COLS = SSD_CONV_DIM + BR_WIDTH + LANES
WB_COLS = 2 * BR_WIDTH
WC_COLS = 3 * BR_WIDTH + 2 * LANES + 2 * BR_WIDTH
WD_COLS = 2 * BR_WIDTH

VMEM_LIMIT = 56 * 1024 * 1024


def _dot(a, b):
    return jnp.dot(a, b, preferred_element_type=F32)


def _dot_nt(a, b):
    return lax.dot_general(a, b, (((1,), (1,)), ((), ())), preferred_element_type=F32)


def _split3(x):
    hi = x.astype(BF16)
    r = x - hi.astype(F32)
    mid = r.astype(BF16)
    lo = (r - mid.astype(F32)).astype(BF16)
    return hi, mid, lo


def _sel_left(sel, x):
    hi, mid, lo = _split3(x)
    return _dot(sel, hi) + _dot(sel, mid) + _dot(sel, lo)


def _sel_right(x, sel):
    hi, mid, lo = _split3(x)
    return _dot(hi, sel) + _dot(mid, sel) + _dot(lo, sel)


def _softplus(x):
    return jnp.maximum(x, 0.0) + jnp.log1p(jnp.exp(-jnp.abs(x)))


def _silu(x):
    return x * jax.nn.sigmoid(x)


def _rms(x, g):
    return x * lax.rsqrt(jnp.mean(x * x, axis=-1, keepdims=True) + EPS) * g


def _seq_masks(nb, L):
    M = nb * L
    ri = lax.broadcasted_iota(jnp.int32, (M, M), 0)
    ci = lax.broadcasted_iota(jnp.int32, (M, M), 1)
    if nb > 1:
        sh = int(math.log2(L))
        same = lax.shift_right_logical(ri, sh) == lax.shift_right_logical(ci, sh)
        causal = same & (ci <= ri)
        upper = same & (ri <= ci)
    else:
        same = None
        causal = ci <= ri
        upper = ri <= ci
    return same, causal, upper


def _valid_masks(nb, L, valid):
    M = nb * L
    rv = (lax.broadcasted_iota(jnp.int32, (M, 1), 0) & (L - 1)) < valid
    cv = (lax.broadcasted_iota(jnp.int32, (1, M), 1) & (L - 1)) < valid
    return rv, cv


def _row_sel(nb, L, b, dtype):
    rows = lax.broadcasted_iota(jnp.int32, (nb * L, 1), 0)
    return (lax.shift_right_logical(rows, int(math.log2(L))) == b).astype(dtype)


def _col_pieces(n_cols, width):
    return [(c0, min(c0 + width, n_cols)) for c0 in range(0, n_cols, width)]


class _Spread:
    def __init__(self, thunks, n_slots):
        self.thunks, self.n_slots, self.calls, self.done = list(thunks), n_slots, 0, 0

    def slot(self):
        self.calls += 1
        while self.done < len(self.thunks) and self.done * self.n_slots < self.calls * len(self.thunks):
            self.thunks[self.done]()
            self.done += 1

    def flush(self):
        while self.done < len(self.thunks):
            self.thunks[self.done]()
            self.done += 1


def _proj_piece(proj_scr, slot, hb, w_ref, c0, c1):
    proj_scr[slot, :, c0:c1] = _dot(hb, w_ref[:, c0:c1])


def _next_chunk_spec(nb, L, nsub, n_steps, n_blocks):
    def index(i, c):
        nxt = jnp.minimum(i * n_steps + c + 1, n_blocks * n_steps - 1)
        return (nxt // n_steps, (nxt % n_steps) * nsub, 0)
    return pl.BlockSpec((nb, L, D_MODEL), index)


def _layer_spec(shape, l):
    return pl.BlockSpec((None,) + shape, lambda *_: (l,) + (0,) * len(shape), pipeline_mode=pl.Buffered(1))


def _state_spec(shape, l):
    return pl.BlockSpec((None,) + shape, lambda i, c: (l, i) + (0,) * (len(shape) - 1))


def _branch_call(body, *, name, grid, in_specs, args, y_spec, y_shape, st_specs, st_shapes, prev, scratch):
    n_in = len(args)
    prev = () if prev is None else tuple(prev)

    def wrapped(*refs):
        body(*refs[:n_in], *refs[n_in + len(prev):])

    outs = pl.pallas_call(
        wrapped,
        grid=grid,
        in_specs=list(in_specs) + [pl.BlockSpec(memory_space=pl.ANY)] * len(prev),
        out_specs=[y_spec] + list(st_specs),
        out_shape=[y_shape] + list(st_shapes),
        scratch_shapes=scratch,
        input_output_aliases={n_in + k: 1 + k for k in range(len(prev))},
        compiler_params=pltpu.CompilerParams(dimension_semantics=("arbitrary", "arbitrary"),
                                             vmem_limit_bytes=VMEM_LIMIT),
        name=name,
    )(*args, *prev)
    return outs[0], tuple(outs[1:])


def _norm_body(x_ref, g_ref, h_ref, htm_ref):
    hb = _rms(x_ref[0], g_ref[...]).astype(BF16)
    h_ref[0] = hb
    htm_ref[...] = hb


def _norm_call(x, g, tm):
    B, T, _ = x.shape
    return pl.pallas_call(
        _norm_body,
        grid=(B, T // tm),
        in_specs=[pl.BlockSpec((1, tm, D_MODEL), lambda b, t: (b, t, 0)),
                  pl.BlockSpec((1, D_MODEL), lambda b, t: (0, 0))],
        out_specs=[pl.BlockSpec((1, tm, D_MODEL), lambda b, t: (b, t, 0)),
                   pl.BlockSpec((tm, D_MODEL), lambda b, t: (t, b))],
        out_shape=[jax.ShapeDtypeStruct((B, T, D_MODEL), BF16),
                   jax.ShapeDtypeStruct((T, B * D_MODEL), BF16)],
        compiler_params=pltpu.CompilerParams(dimension_semantics=("arbitrary", "arbitrary")),
        name="rmsnorm_in",
    )(x, g)


def _memkv_body(mem_ref, g_ref, w_ref, mk_ref, mv_ref, mk5_ref, mv5_ref):
    hb = _rms(mem_ref[0], g_ref[0]).astype(BF16)
    kv = _dot(hb, w_ref[0])
    mk_ref[0, 0] = kv[:, :BR_WIDTH]
    mv_ref[0, 0] = kv[:, BR_WIDTH:]
    for hd in range(XA_HEADS):
        lo, hi = hd * XA_HEAD_DIM, (hd + 1) * XA_HEAD_DIM
        mk5_ref[0, 0, :, hd, :] = kv[:, lo:hi]
        mv5_ref[0, 0, :, hd, :] = kv[:, BR_WIDTH + lo:BR_WIDTH + hi]


def _memkv_call(mem, g, w_kv):
    B = mem.shape[0]
    out = jax.ShapeDtypeStruct((DEPTH, B, MEM_TOKENS, BR_WIDTH), F32)
    out5 = jax.ShapeDtypeStruct((DEPTH, B, MEM_TOKENS, XA_HEADS, XA_HEAD_DIM), F32)
    spec5 = pl.BlockSpec((1, 1, MEM_TOKENS, XA_HEADS, XA_HEAD_DIM), lambda l, b: (l, b, 0, 0, 0))
    return pl.pallas_call(
        _memkv_body,
        grid=(DEPTH, B),
        in_specs=[pl.BlockSpec((1, MEM_TOKENS, D_MODEL), lambda l, b: (b, 0, 0)),
                  pl.BlockSpec((1, 1, D_MODEL), lambda l, b: (l, 0, 0)),
                  pl.BlockSpec((1, D_MODEL, 2 * BR_WIDTH), lambda l, b: (l, 0, 0))],
        out_specs=[pl.BlockSpec((1, 1, MEM_TOKENS, BR_WIDTH), lambda l, b: (l, b, 0, 0)),
                   pl.BlockSpec((1, 1, MEM_TOKENS, BR_WIDTH), lambda l, b: (l, b, 0, 0)), spec5, spec5],
        out_shape=[out, out, out5, out5],
        compiler_params=pltpu.CompilerParams(dimension_semantics=("arbitrary", "arbitrary"),
                                             vmem_limit_bytes=VMEM_LIMIT),
        name="mem_kv",
    )(mem, g, w_kv)


def _ssd_body(h_ref, hn_ref, w_ref, wst_ref, prow_ref, pcol_ref, cw_ref, cb_ref, dsk_ref, nrm_ref, buf0_ref,
              s0_ref, y_ref, bufo_ref, so_ref, xp_scr, yin_scr, yst_scr, proj_scr, *, nb, L, valid, nsub):
    M = nb * L
    pipelined = nsub % 2 == 0
    chunk = lambda ref, s: ref[:, s * L:(s + 1) * L, :].reshape(M, D_MODEL)

    @pl.when(pl.program_id(1) == 0)
    def _init():
        so_ref[...] = s0_ref[...]
        xp_scr[:, CONV_PAD - 3:CONV_PAD, :] = buf0_ref[...]

    if pipelined:
        @pl.when((pl.program_id(0) == 0) & (pl.program_id(1) == 0))
        def _prologue():
            proj_scr[0] = _dot(chunk(h_ref, 0), w_ref[...])
    else:
        assert nsub == 1
        proj_scr[0] = _dot(chunk(h_ref, 0), w_ref[...])

    for s in range(nsub):
        hb_next = chunk(h_ref, s + 1) if s + 1 < nsub else (hn_ref[...].reshape(M, D_MODEL) if pipelined else None)
        thunks = [] if hb_next is None else [
            functools.partial(_proj_piece, proj_scr, (s + 1) % 2, hb_next, w_ref, c0, c1)
            for c0, c1 in _col_pieces(WA_COLS, 512)]
        _ssd_chunk(chunk(h_ref, s), proj_scr.at[s % 2], s, _Spread(thunks, SSD_HEADS * (1 + nb)),
                   wst_ref, prow_ref, pcol_ref, cw_ref, cb_ref, dsk_ref, nrm_ref,
                   y_ref, bufo_ref, so_ref, xp_scr, yin_scr, yst_scr, nb=nb, L=L, valid=valid)


def _ssd_chunk(hb, proj, s, spread, wst_ref, prow_ref, pcol_ref, cw_ref, cb_ref, dsk_ref, nrm_ref,
               y_ref, bufo_ref, so_ref, xp_scr, yin_scr, yst_scr, *, nb, L, valid):
    M = nb * L
    hg = SSD_HEADS // SSD_GROUPS

    xp_scr[:, CONV_PAD:CONV_PAD + L, :] = proj[:, :SSD_CONV_DIM].reshape(nb, L, SSD_CONV_DIM)
    conv = cb_ref[...][None]
    for k in range(SSD_CONV):
        o = CONV_PAD - 3 + k
        conv = conv + xp_scr[:, o:o + L, :] * cw_ref[k:k + 1, :][None]
    nbuf = xp_scr[:, CONV_PAD - 3 + valid:CONV_PAD + valid, :]
    xp_scr[:, CONV_PAD - 3:CONV_PAD, :] = nbuf
    bufo_ref[...] = nbuf
    xbc = _silu(conv).reshape(M, SSD_CONV_DIM)
    xs = xbc[:, :BR_WIDTH]
    bm = xbc[:, BR_WIDTH:BR_WIDTH + SSD_GROUPS * SSD_STATE].astype(BF16)
    cm = xbc[:, BR_WIDTH + SSD_GROUPS * SSD_STATE:].astype(BF16)
    z = proj[:, SSD_CONV_DIM:SSD_CONV_DIM + BR_WIDTH]

    same, causal, upper = _seq_masks(nb, L)
    tril_b = causal.astype(BF16)
    triu_b = upper.astype(BF16)
    same_b = jnp.ones((M, M), BF16) if same is None else same.astype(BF16)

    dt = _softplus(proj[:, SSD_CONV_DIM + BR_WIDTH:] + prow_ref[0:1, :])
    dtT = _softplus(_dot_nt(wst_ref[...], hb) + pcol_ref[:, 0:1])
    dt = jnp.where(lax.broadcasted_iota(jnp.int32, (1, LANES), 1) < SSD_HEADS, dt, 0.0)
    if valid < L:
        rv, cv = _valid_masks(nb, L, valid)
        dt = jnp.where(rv, dt, 0.0)
        dtT = jnp.where(cv, dtT, 0.0)
    la = dt * (-jnp.exp(prow_ref[1:2, :]))
    laT = dtT * (-jnp.exp(pcol_ref[:, 1:2]))
    acs = _sel_left(tril_b, la)
    tot = _sel_left(same_b, la)
    acsT = _sel_right(laT, triu_b)

    totT = _sel_right(laT, same_b)
    tailT = jnp.exp(totT - acsT)
    eend = jnp.exp(tot)
    expand = (lax.shift_right_logical(lax.broadcasted_iota(jnp.int32, (LANES, BR_WIDTH), 1), 6)
              == lax.broadcasted_iota(jnp.int32, (LANES, BR_WIDTH), 0)).astype(BF16)
    eacs_x = _sel_right(jnp.exp(acs), expand)

    xsT = xs.T
    wT = dtT * tailT

    for g in range(SSD_GROUPS):
        cm_g = cm[:, g * SSD_STATE:(g + 1) * SSD_STATE]
        bm_g = bm[:, g * SSD_STATE:(g + 1) * SSD_STATE]
        cb = _dot_nt(cm_g, bm_g)
        ats = []
        for hh in range(hg):
            h = g * hg + hh
            lo, hi = h * SSD_HEAD_DIM, (h + 1) * SSD_HEAD_DIM
            seg = acs[:, h:h + 1] - acsT[h:h + 1, :]
            decay = jnp.exp(jnp.where(causal, seg, NEG_INF))
            xdtT = (xsT[lo:hi, :] * dtT[h:h + 1, :]).astype(BF16)
            ats.append((xsT[lo:hi, :] * wT[h:h + 1, :]).astype(BF16))
            yin_scr[:, lo:hi] = _dot_nt((cb * decay).astype(BF16), xdtT)
            spread.slot()
        for b in range(nb):
            r0, r1 = b * L, (b + 1) * L
            bm_b = bm_g if nb == 1 else bm_g * _row_sel(nb, L, b, BF16)
            for hh in range(hg):
                h = g * hg + hh
                lo, hi = h * SSD_HEAD_DIM, (h + 1) * SSD_HEAD_DIM
                s_old = so_ref[b, h]
                yst_scr[r0:r1, lo:hi] = _dot_nt(cm_g[r0:r1, :], s_old.astype(BF16))
                so_ref[b, h] = eend[r0:r0 + 1, h:h + 1] * s_old + _dot(ats[hh], bm_b)
                spread.slot()
    spread.flush()

    y = yin_scr[...] + eacs_x * yst_scr[...] + dsk_ref[...] * xs
    y_ref[:, s * L:(s + 1) * L, :] = _rms(y * _silu(z), nrm_ref[...]).reshape(nb, L, BR_WIDTH).astype(BF16)


def _ssd_call(h, p, buf0, s0, l, ls, prev, *, nb, L, valid, nsub):
    B, T, _ = h.shape
    wspec = lambda shape: _layer_spec(shape, l)
    conv_blk = (nb, SSD_CONV - 1, SSD_CONV_DIM)
    ssd_blk = (nb, SSD_HEADS, SSD_HEAD_DIM, SSD_STATE)
    return _branch_call(
        functools.partial(_ssd_body, nb=nb, L=L, valid=valid, nsub=nsub),
        name="ssd_branch",
        grid=(B // nb, T // (nsub * L)),
        in_specs=[pl.BlockSpec((nb, nsub * L, D_MODEL), lambda i, c: (i, c, 0)),
                  _next_chunk_spec(nb, L, nsub, T // (nsub * L), B // nb),
                  wspec((D_MODEL, WA_COLS)), wspec((SSD_HEADS, D_MODEL)),
                  wspec((8, LANES)), wspec((SSD_HEADS, 2)),
                  wspec((SSD_CONV, SSD_CONV_DIM)), wspec((1, SSD_CONV_DIM)),
                  wspec((1, BR_WIDTH)), wspec((1, BR_WIDTH)),
                  _state_spec(conv_blk, ls), _state_spec(ssd_blk, ls)],
        args=(h, h, p["wA"], p["wstA"], p["prowA"], p["pcolA"], p["conv_w"], p["conv_b"], p["dskA"], p["nrmA"],
              buf0, s0),
        y_spec=pl.BlockSpec((nb, nsub * L, BR_WIDTH), lambda i, c: (i, c, 0)),
        y_shape=jax.ShapeDtypeStruct((B, T, BR_WIDTH), BF16),
        st_specs=[_state_spec(conv_blk, l), _state_spec(ssd_blk, l)],
        st_shapes=[jax.ShapeDtypeStruct((DEPTH, B) + conv_blk[1:], F32),
                   jax.ShapeDtypeStruct((DEPTH, B) + ssd_blk[1:], F32)],
        prev=prev,
        scratch=[pltpu.VMEM((nb, L + CONV_PAD, SSD_CONV_DIM), F32),
                 pltpu.VMEM((nb * L, BR_WIDTH), F32),
                 pltpu.VMEM((nb * L, BR_WIDTH), F32),
                 pltpu.VMEM((2, nb * L, WA_COLS), F32)])


S5_SCAN_LANES = 512


def _s5_body(h_ref, w_ref, bc_ref, cc_ref, lam_ref, dsk_ref, gw_ref, gb_ref, sre0_ref, sim0_ref,
             y_ref, sre_ref, sim_ref, hs_scr, st_scr, *, nb, Lt):
    half = S5_CLUSTER_GROUPS * S5_STATE

    @pl.when(pl.program_id(1) == 0)
    def _load_state():
        for g in range(S5_GROUPS):
            st_scr[0, :, g * S5_STATE:(g + 1) * S5_STATE] = sre0_ref[pl.ds(g, nb, stride=S5_GROUPS), :]
            st_scr[1, :, g * S5_STATE:(g + 1) * S5_STATE] = sim0_ref[pl.ds(g, nb, stride=S5_GROUPS), :]

    hb = h_ref[0]
    proj = _dot(hb, w_ref[...])
    u = proj[:, :BR_WIDTH]
    z = proj[:, BR_WIDTH:]
    ub = u.astype(BF16)
    gl = S5_CLUSTER_GROUPS * S5_GROUP
    ys = []
    for j in range(S5_CLUSTERS):
        hs_scr[:, 2 * half * j:2 * half * (j + 1)] = _dot(ub[:, gl * j:gl * (j + 1)], bc_ref[j])
        for q in range(half // S5_SCAN_LANES):
            cr = 2 * half * j + S5_SCAN_LANES * q
            ci = cr + half
            sc = half * j + S5_SCAN_LANES * q
            lr = jnp.broadcast_to(lam_ref[0:1, sc:sc + S5_SCAN_LANES], (8, S5_SCAN_LANES))
            li = jnp.broadcast_to(lam_ref[1:2, sc:sc + S5_SCAN_LANES], (8, S5_SCAN_LANES))
            for sg in range(nb // 8):
                hr = st_scr[0, 8 * sg:8 * sg + 8, sc:sc + S5_SCAN_LANES]
                hi = st_scr[1, 8 * sg:8 * sg + 8, sc:sc + S5_SCAN_LANES]
                for t in range(Lt):
                    r0 = t * nb + 8 * sg
                    nr = lr * hr - li * hi + hs_scr[r0:r0 + 8, cr:cr + S5_SCAN_LANES]
                    ni = lr * hi + li * hr + hs_scr[r0:r0 + 8, ci:ci + S5_SCAN_LANES]
                    hs_scr[r0:r0 + 8, cr:cr + S5_SCAN_LANES] = nr
                    hs_scr[r0:r0 + 8, ci:ci + S5_SCAN_LANES] = ni
                    hr, hi = nr, ni
                st_scr[0, 8 * sg:8 * sg + 8, sc:sc + S5_SCAN_LANES] = hr
                st_scr[1, 8 * sg:8 * sg + 8, sc:sc + S5_SCAN_LANES] = hi
        ys.append(_dot(hs_scr[:, 2 * half * j:2 * half * (j + 1)].astype(BF16), cc_ref[j]))

    y = jnp.concatenate(ys, axis=1) + dsk_ref[...] * u
    yb = jax.nn.gelu(y)
    glu = jax.nn.sigmoid(_dot(yb.astype(BF16), gw_ref[...]) + gb_ref[...])
    y_ref[0] = (yb * glu * _silu(z)).astype(BF16)

    @pl.when(pl.program_id(1) == pl.num_programs(1) - 1)
    def _store_state():
        for g in range(S5_GROUPS):
            sre_ref[pl.ds(g, nb, stride=S5_GROUPS), :] = st_scr[0, :, g * S5_STATE:(g + 1) * S5_STATE]
            sim_ref[pl.ds(g, nb, stride=S5_GROUPS), :] = st_scr[1, :, g * S5_STATE:(g + 1) * S5_STATE]


def _s5_call(h_tm, p, sre0, sim0, l, ls, prev, *, nb, Lt):
    NB, R, _ = h_tm.shape
    T = R // nb
    wspec = lambda shape: _layer_spec(shape, l)
    st_blk = (nb * S5_GROUPS, S5_STATE)
    st_shape = jax.ShapeDtypeStruct((DEPTH, NB * nb * S5_GROUPS, S5_STATE), F32)
    return _branch_call(
        functools.partial(_s5_body, nb=nb, Lt=Lt),
        name="s5_branch",
        grid=(NB, T // Lt),
        in_specs=[pl.BlockSpec((1, nb * Lt, D_MODEL), lambda i, c: (i, c, 0)),
                  wspec((D_MODEL, WB_COLS)),
                  wspec((S5_CLUSTERS, S5_CLUSTER_GROUPS * S5_GROUP, 2 * S5_CLUSTER_GROUPS * S5_STATE)),
                  wspec((S5_CLUSTERS, 2 * S5_CLUSTER_GROUPS * S5_STATE, S5_CLUSTER_GROUPS * S5_GROUP)),
                  wspec((2, S5_FLAT)), wspec((1, BR_WIDTH)),
                  wspec((BR_WIDTH, BR_WIDTH)), wspec((1, BR_WIDTH)),
                  _state_spec(st_blk, ls), _state_spec(st_blk, ls)],
        args=(h_tm, p["wB"], p["bc"], p["cc"], p["lam"], p["dskB"], p["glu_w"], p["glu_b"], sre0, sim0),
        y_spec=pl.BlockSpec((1, nb * Lt, BR_WIDTH), lambda i, c: (i, c, 0)),
        y_shape=jax.ShapeDtypeStruct((NB, R, BR_WIDTH), BF16),
        st_specs=[_state_spec(st_blk, l), _state_spec(st_blk, l)],
        st_shapes=[st_shape, st_shape],
        prev=prev,
        scratch=[pltpu.VMEM((nb * Lt, 2 * S5_FLAT), F32), pltpu.VMEM((2, nb, S5_FLAT), F32)])


def _mlstm_body(h_ref, hn_ref, w_ref, wst_ref, prow_ref, pcol_ref, nrm_ref, c0_ref, n0_ref, m0_ref,
                y_ref, co_ref, no_ref, mo_ref, num_scr, proj_scr, *, nb, L, valid, nsub):
    M = nb * L
    pipelined = nsub % 2 == 0
    chunk = lambda ref, s: ref[:, s * L:(s + 1) * L, :].reshape(M, D_MODEL)

    @pl.when(pl.program_id(1) == 0)
    def _init():
        co_ref[...] = c0_ref[...]
        no_ref[...] = n0_ref[...]
        mo_ref[...] = m0_ref[...]

    if pipelined:
        @pl.when((pl.program_id(0) == 0) & (pl.program_id(1) == 0))
        def _prologue():
            proj_scr[0] = _dot(chunk(h_ref, 0), w_ref[...])
    else:
        assert nsub == 1
        proj_scr[0] = _dot(chunk(h_ref, 0), w_ref[...])

    for s in range(nsub):
        hb_next = chunk(h_ref, s + 1) if s + 1 < nsub else (hn_ref[...].reshape(M, D_MODEL) if pipelined else None)
        thunks = [] if hb_next is None else [
            functools.partial(_proj_piece, proj_scr, (s + 1) % 2, hb_next, w_ref, c0, c1)
            for c0, c1 in _col_pieces(WC_COLS, 768)]
        _mlstm_chunk(chunk(h_ref, s), proj_scr.at[s % 2], s, _Spread(thunks, ML_HEADS * (1 + nb)),
                     wst_ref, prow_ref, pcol_ref, nrm_ref, y_ref, co_ref, no_ref, mo_ref, num_scr,
                     nb=nb, L=L, valid=valid)


def _mlstm_chunk(hb, proj, s, spread, wst_ref, prow_ref, pcol_ref, nrm_ref, y_ref, co_ref, no_ref, mo_ref,
                 num_scr, *, nb, L, valid):
    M = nb * L
    W = BR_WIDTH
    q = proj[:, :W]
    k = proj[:, W:2 * W] * (ML_HEAD_DIM ** -0.5)
    v = proj[:, 2 * W:3 * W]
    ig = proj[:, 3 * W:3 * W + LANES] + prow_ref[0:1, :]
    logf = -_softplus(-(proj[:, 3 * W + LANES:3 * W + 2 * LANES] + prow_ref[1:2, :]))
    o = proj[:, 3 * W + 2 * LANES:4 * W + 2 * LANES]
    z = proj[:, 4 * W + 2 * LANES:]
    smallT = _dot_nt(wst_ref[...], hb) + pcol_ref[...]
    igT = smallT[0:8, :]
    logfT = -_softplus(-smallT[8:16, :])
    if valid < L:
        rv, cv = _valid_masks(nb, L, valid)
        ig = jnp.where(rv, ig, NEG_INF)
        logf = jnp.where(rv, logf, 0.0)
        igT = jnp.where(cv, igT, NEG_INF)
        logfT = jnp.where(cv, logfT, 0.0)

    same, causal, upper = _seq_masks(nb, L)
    tril_b = causal.astype(BF16)
    triu_b = upper.astype(BF16)
    same_b = jnp.ones((M, M), BF16) if same is None else same.astype(BF16)
    bcum = _sel_left(tril_b, logf)
    tot = _sel_left(same_b, logf)
    bcumT = _sel_right(logfT, triu_b)
    totT = _sel_right(logfT, same_b)
    m_rows = jnp.broadcast_to(mo_ref[...], (nb, L, LANES)).reshape(M, LANES)
    gcar = bcum + m_rows
    g_end = tot + m_rows
    d_end = tot - bcum + ig
    d_endT = totT - bcumT + igT

    qb = q.astype(BF16)
    kb = k.astype(BF16)
    vb = v.astype(BF16)
    lane = lax.broadcasted_iota(jnp.int32, (M, LANES), 1)
    m_new_all = jnp.zeros((M, LANES), F32)
    per_head = []
    wk_parts = []
    for hd in range(ML_HEADS):
        lo, hi = hd * ML_HEAD_DIM, (hd + 1) * ML_HEAD_DIM
        dmat = jnp.where(causal, bcum[:, hd:hd + 1] - bcumT[hd:hd + 1, :] + igT[hd:hd + 1, :], NEG_INF)
        m_l = jnp.maximum(gcar[:, hd:hd + 1], jnp.max(dmat, axis=1, keepdims=True))
        w_inter = jnp.exp(gcar[:, hd:hd + 1] - m_l)
        qk = _dot_nt(qb[:, lo:hi], kb[:, lo:hi]) * jnp.exp(dmat - m_l)
        num_scr[:, lo:hi] = _dot(qk.astype(BF16), vb[:, lo:hi])
        den_intra = jnp.sum(qk, axis=1, keepdims=True)
        d_row = d_endT[hd:hd + 1, :] if same is None else jnp.where(same, d_endT[hd:hd + 1, :], NEG_INF)
        m_new = jnp.maximum(g_end[:, hd:hd + 1], jnp.max(d_row, axis=1, keepdims=True))
        w_end = jnp.exp(d_end[:, hd:hd + 1] - m_new)
        dec = jnp.exp(g_end[:, hd:hd + 1] - m_new)
        wk_parts.append(w_end * k[:, lo:hi])
        m_new_all = jnp.where(lane == hd, m_new, m_new_all)
        per_head.append((m_l, w_inter, den_intra, dec))
        spread.slot()
    wk = jnp.concatenate(wk_parts, axis=1)
    kt = wk.T.astype(BF16)

    for b in range(nb):
        r0, r1 = b * L, (b + 1) * L
        sel = None if nb == 1 else _row_sel(nb, L, b, BF16)
        for hd in range(ML_HEADS):
            lo, hi = hd * ML_HEAD_DIM, (hd + 1) * ML_HEAD_DIM
            m_l, w_inter, den_intra, dec = per_head[hd]
            c_old = co_ref[b, hd]
            n_old = no_ref[b, hd:hd + 1, :]
            q_r = q[r0:r1, lo:hi]
            wi = w_inter[r0:r1]
            num = num_scr[r0:r1, lo:hi] + wi * _dot(qb[r0:r1, lo:hi], c_old.astype(BF16))
            den = den_intra[r0:r1] + wi * jnp.sum(q_r * n_old, axis=1, keepdims=True)
            hout = num / jnp.maximum(jnp.abs(den), jnp.exp(-m_l[r0:r1]))
            ho = jax.nn.sigmoid(o[r0:r1, lo:hi]) * hout
            yv = _rms(ho, nrm_ref[:, lo:hi]) * _silu(z[r0:r1, lo:hi])
            y_ref[b, s * L:(s + 1) * L, lo:hi] = yv.astype(BF16)
            dec_b = dec[r0:r0 + 1]
            vb_b = vb[:, lo:hi] if sel is None else vb[:, lo:hi] * sel
            co_ref[b, hd] = dec_b * c_old + _dot(kt[lo:hi, :], vb_b)
            no_ref[b, hd:hd + 1, :] = dec_b * n_old + jnp.sum(wk[r0:r1, lo:hi], axis=0, keepdims=True)
            spread.slot()
        mo_ref[b] = m_new_all[r0:r0 + 1, :]
    spread.flush()


def _mlstm_call(h, p, c0, n0, m0, l, ls, prev, *, nb, L, valid, nsub):
    B, T, _ = h.shape
    wspec = lambda shape: _layer_spec(shape, l)
    c_blk = (nb, ML_HEADS, ML_HEAD_DIM, ML_HEAD_DIM)
    n_blk = (nb, ML_HEADS, ML_HEAD_DIM)
    m_blk = (nb, 1, LANES)
    return _branch_call(
        functools.partial(_mlstm_body, nb=nb, L=L, valid=valid, nsub=nsub),
        name="mlstm_branch",
        grid=(B // nb, T // (nsub * L)),
        in_specs=[pl.BlockSpec((nb, nsub * L, D_MODEL), lambda i, c: (i, c, 0)),
                  _next_chunk_spec(nb, L, nsub, T // (nsub * L), B // nb),
                  wspec((D_MODEL, WC_COLS)), wspec((16, D_MODEL)),
                  wspec((8, LANES)), wspec((16, 1)), wspec((1, BR_WIDTH)),
                  _state_spec(c_blk, ls), _state_spec(n_blk, ls), _state_spec(m_blk, ls)],
        args=(h, h, p["wC"], p["wstC"], p["prowC"], p["pcolC"], p["nrmC"], c0, n0, m0),
        y_spec=pl.BlockSpec((nb, nsub * L, BR_WIDTH), lambda i, c: (i, c, 0)),
        y_shape=jax.ShapeDtypeStruct((B, T, BR_WIDTH), BF16),
        st_specs=[_state_spec(c_blk, l), _state_spec(n_blk, l), _state_spec(m_blk, l)],
        st_shapes=[jax.ShapeDtypeStruct((DEPTH, B) + c_blk[1:], F32),
                   jax.ShapeDtypeStruct((DEPTH, B) + n_blk[1:], F32),
                   jax.ShapeDtypeStruct((DEPTH, B) + m_blk[1:], F32)],
        prev=prev,
        scratch=[pltpu.VMEM((nb * L, BR_WIDTH), F32), pltpu.VMEM((2, nb * L, WC_COLS), F32)])


def _xattn_body(h_ref, w_ref, mk_ref, mv_ref, y_ref, *, nb, L):
    M = nb * L
    hb = h_ref[...].reshape(M, D_MODEL)
    proj = _dot(hb, w_ref[...])
    qb = proj[:, :BR_WIDTH].astype(BF16)
    z = proj[:, BR_WIDTH:]
    for b in range(nb):
        r0, r1 = b * L, (b + 1) * L
        for hd in range(XA_HEADS):
            lo, hi = hd * XA_HEAD_DIM, (hd + 1) * XA_HEAD_DIM
            s = _dot_nt(qb[r0:r1, lo:hi], mk_ref[b, :, lo:hi].astype(BF16)) * (XA_HEAD_DIM ** -0.5)
            e = jnp.exp(s - jnp.max(s, axis=1, keepdims=True))
            p = e / jnp.sum(e, axis=1, keepdims=True)
            a = _dot(p.astype(BF16), mv_ref[b, :, lo:hi].astype(BF16))
            y_ref[b, :, lo:hi] = (a * _silu(z[r0:r1, lo:hi])).astype(BF16)


def _xattn_rows_body(h_ref, w_ref, mk_ref, mv_ref, y_ref, *, nb, L):
    M = nb * L
    R = MEM_TOKENS * XA_HEADS
    hb = h_ref[...].reshape(M, D_MODEL)
    proj = _dot(hb, w_ref[...])
    q = proj[:, :BR_WIDTH]
    z = proj[:, BR_WIDTH:]
    row_head = lax.broadcasted_iota(jnp.int32, (R, LANES), 0) & (XA_HEADS - 1)
    col_head = lax.shift_right_logical(lax.broadcasted_iota(jnp.int32, (R, LANES), 1), int(math.log2(L)))
    own = row_head == col_head
    zpad = jnp.zeros((LANES - XA_HEADS * L, XA_HEAD_DIM), F32)
    for b in range(nb):
        r0, r1 = b * L, (b + 1) * L
        qall = jnp.concatenate([q[r0:r1, hd * XA_HEAD_DIM:(hd + 1) * XA_HEAD_DIM] for hd in range(XA_HEADS)]
                               + [zpad], axis=0).astype(BF16)
        mk2 = mk_ref[b].reshape(R, XA_HEAD_DIM).astype(BF16)
        mv2 = mv_ref[b].reshape(R, XA_HEAD_DIM).astype(BF16)
        s = jnp.where(own, _dot_nt(mk2, qall) * (XA_HEAD_DIM ** -0.5), -1e30)
        e = jnp.exp(s - jnp.max(s, axis=0, keepdims=True))
        pr = e / jnp.sum(e, axis=0, keepdims=True)
        a = _dot(pr.T.astype(BF16), mv2)
        for hd in range(XA_HEADS):
            lo, hi = hd * XA_HEAD_DIM, (hd + 1) * XA_HEAD_DIM
            y_ref[b, :, lo:hi] = (a[hd * L:(hd + 1) * L, :] * _silu(z[r0:r1, lo:hi])).astype(BF16)


def _xattn_rows_call(h, wD, mk, mv, l, lkv, *, nb, L):
    B, T, _ = h.shape
    assert XA_HEADS * L <= LANES
    kv_spec = pl.BlockSpec((None, nb, MEM_TOKENS, XA_HEADS, XA_HEAD_DIM), lambda i, c: (lkv, i, 0, 0, 0))
    return pl.pallas_call(
        functools.partial(_xattn_rows_body, nb=nb, L=L),
        grid=(B // nb, T // L),
        in_specs=[pl.BlockSpec((nb, L, D_MODEL), lambda i, c: (i, c, 0)),
                  _layer_spec((D_MODEL, WD_COLS), l),
                  kv_spec, kv_spec],
        out_specs=pl.BlockSpec((nb, L, BR_WIDTH), lambda i, c: (i, c, 0)),
        out_shape=jax.ShapeDtypeStruct((B, T, BR_WIDTH), BF16),
        compiler_params=pltpu.CompilerParams(dimension_semantics=("arbitrary", "arbitrary"),
                                             vmem_limit_bytes=VMEM_LIMIT),
        name="xattn_rows",
    )(h, wD, mk, mv)


def _xattn_call(h, wD, mk, mv, l, lkv, *, nb, L):
    B, T, _ = h.shape
    grid = (B // nb, T // L)
    kv_spec = pl.BlockSpec((None, nb, MEM_TOKENS, BR_WIDTH), lambda i, c: (lkv, i, 0, 0))
    body = functools.partial(_xattn_body, nb=nb, L=L)
    return pl.pallas_call(
        body,
        grid=grid,
        in_specs=[pl.BlockSpec((nb, L, D_MODEL), lambda i, c: (i, c, 0)),
                  _layer_spec((D_MODEL, WD_COLS), l),
                  kv_spec, kv_spec],
        out_specs=pl.BlockSpec((nb, L, BR_WIDTH), lambda i, c: (i, c, 0)),
        out_shape=jax.ShapeDtypeStruct((B, T, BR_WIDTH), BF16),
        compiler_params=pltpu.CompilerParams(dimension_semantics=("arbitrary", "arbitrary"),
                                             vmem_limit_bytes=VMEM_LIMIT),
        name="xattn_branch",
    )(h, wD, mk, mv)


def _merge_body(h_ref, ya_ref, yb_ref, yc_ref, yd_ref, x_ref, wg_ref, bg_ref, wd_ref, wo_ref, gn_ref, *outs, last):
    h = h_ref[0]
    ys = (ya_ref[0], yb_ref[...], yc_ref[0], yd_ref[0])
    merged = None
    for kbr in range(4):
        lo, hi = kbr * D_MODEL, (kbr + 1) * D_MODEL
        gate = jax.nn.sigmoid(_dot(h, wg_ref[:, lo:hi]) + bg_ref[:, lo:hi])
        term = gate * _dot(ys[kbr], wd_ref[kbr])
        merged = term if merged is None else merged + term
    xn = x_ref[0] + _dot(merged.astype(BF16), wo_ref[...])
    if last:
        outs[0][0] = _rms(xn, gn_ref[...])
    else:
        outs[0][0] = xn
        hb = _rms(xn, gn_ref[...]).astype(BF16)
        outs[1][0] = hb
        outs[2][...] = hb


def _merge_call(h, ya, yb_tm, yc, yd, x, wE, b_gate, w_down, w_out, g_next, l, *, tm, last):
    B, T, _ = x.shape
    row = pl.BlockSpec((1, tm, D_MODEL), lambda b, t: (b, t, 0))
    tmaj = pl.BlockSpec((tm, D_MODEL), lambda b, t: (t, b))
    wspec = lambda shape: _layer_spec(shape, l)
    if last:
        out_specs = [row]
        out_shape = [jax.ShapeDtypeStruct((B, T, D_MODEL), F32)]
    else:
        out_specs = [row, row, tmaj]
        out_shape = [jax.ShapeDtypeStruct((B, T, D_MODEL), F32),
                     jax.ShapeDtypeStruct((B, T, D_MODEL), BF16),
                     jax.ShapeDtypeStruct((T, B * D_MODEL), BF16)]
    return pl.pallas_call(
        functools.partial(_merge_body, last=last),
        grid=(B, T // tm),
        in_specs=[row, row, tmaj, row, row, row,
                  wspec((D_MODEL, 4 * D_MODEL)), wspec((1, 4 * D_MODEL)),
                  wspec((4, BR_WIDTH, D_MODEL)), wspec((D_MODEL, D_MODEL)),
                  pl.BlockSpec((1, D_MODEL), lambda b, t: (0, 0))],
        out_specs=out_specs,
        out_shape=out_shape,
        compiler_params=pltpu.CompilerParams(dimension_semantics=("arbitrary", "arbitrary"),
                                             vmem_limit_bytes=VMEM_LIMIT),
        name="merge_out",
    )(h, ya, yb_tm, yc, yd, x, wE, b_gate, w_down, w_out, g_next)


def _pack_params(w_in, b_gate, b_igate, b_fgate, ssd_conv_w, ssd_conv_b, ssd_dt_bias, ssd_a_log, ssd_d, ssd_norm,
                 s5_a_re, s5_a_im, s5_log_dt, s5_b_re, s5_b_im, s5_c_re, s5_c_im, s5_d, s5_glu_w, s5_glu_b,
                 ml_norm, w_down, w_out):
    offs = {}
    acc = 0
    for name, size in zip(IN_NAMES, IN_SIZES):
        offs[name] = (acc, acc + size)
        acc += size
    col = lambda name: w_in[:, :, offs[name][0]:offs[name][1]]
    padl = lambda a: jnp.pad(a, ((0, 0), (0, 0), (0, LANES - a.shape[-1])))
    p = {}
    p["wA"] = jnp.concatenate([col("xbc"), col("z_ssd"), padl(col("dt"))], axis=-1).astype(BF16)
    p["wstA"] = jnp.swapaxes(col("dt"), 1, 2).astype(BF16)
    p["wB"] = jnp.concatenate([col("u_s5"), col("z_s5")], axis=-1).astype(BF16)
    p["wC"] = jnp.concatenate([col("q"), col("k"), col("v"), padl(col("i")), padl(col("f")), col("o"),
                               col("z_ml")], axis=-1).astype(BF16)
    zrow = jnp.zeros((DEPTH, 4, D_MODEL), F32)
    p["wstC"] = jnp.concatenate([jnp.swapaxes(col("i"), 1, 2), zrow, jnp.swapaxes(col("f"), 1, 2), zrow],
                                axis=1).astype(BF16)
    p["wD"] = jnp.concatenate([col("q_xa"), col("z_xa")], axis=-1).astype(BF16)
    p["wE"] = col("gate").astype(BF16)
    p["b_gate"] = b_gate[:, None, :]
    p["w_down"] = w_down.astype(BF16)
    p["w_out"] = w_out.astype(BF16)

    padv = lambda a: jnp.pad(a, ((0, 0), (0, LANES - a.shape[-1])))
    zl = jnp.zeros((DEPTH, 6, LANES), F32)
    p["prowA"] = jnp.concatenate([padv(ssd_dt_bias)[:, None], padv(ssd_a_log)[:, None], zl], axis=1)
    p["pcolA"] = jnp.stack([ssd_dt_bias, ssd_a_log], axis=-1)
    p["conv_w"] = ssd_conv_w
    p["conv_b"] = ssd_conv_b[:, None, :]
    p["dskA"] = jnp.repeat(ssd_d, SSD_HEAD_DIM, axis=-1)[:, None, :]
    p["nrmA"] = ssd_norm[:, None, :]

    p["prowC"] = jnp.concatenate([padv(b_igate)[:, None], padv(b_fgate)[:, None], zl], axis=1)
    z4 = jnp.zeros((DEPTH, 4), F32)
    p["pcolC"] = jnp.concatenate([b_igate, z4, b_fgate, z4], axis=1)[:, :, None]
    p["nrmC"] = ml_norm.reshape(DEPTH, 1, BR_WIDTH)

    dt = jnp.exp(s5_log_dt)[:, :, None]
    mag = jnp.exp(s5_a_re * dt)
    lr = mag * jnp.cos(s5_a_im * dt)
    li = mag * jnp.sin(s5_a_im * dt)
    den = s5_a_re * s5_a_re + s5_a_im * s5_a_im
    cr = ((lr - 1.0) * s5_a_re + li * s5_a_im) / den
    ci = (li * s5_a_re - (lr - 1.0) * s5_a_im) / den
    bb_re = cr[..., None] * s5_b_re - ci[..., None] * s5_b_im
    bb_im = cr[..., None] * s5_b_im + ci[..., None] * s5_b_re
    eye = jnp.eye(S5_CLUSTER_GROUPS, dtype=F32)
    cg = S5_CLUSTER_GROUPS

    def pack_b(bb):
        bb = bb.reshape(DEPTH, S5_CLUSTERS, cg, S5_STATE, S5_GROUP)
        return jnp.einsum("ljgnc,gh->ljgchn", bb, eye).reshape(DEPTH, S5_CLUSTERS, cg * S5_GROUP, cg * S5_STATE)

    def pack_c(cc):
        cc = cc.reshape(DEPTH, S5_CLUSTERS, cg, S5_GROUP, S5_STATE)
        return jnp.einsum("ljgcn,gh->ljgnhc", cc, eye).reshape(DEPTH, S5_CLUSTERS, cg * S5_STATE, cg * S5_GROUP)

    p["bc"] = jnp.concatenate([pack_b(bb_re), pack_b(bb_im)], axis=-1).astype(BF16)
    p["cc"] = jnp.concatenate([pack_c(s5_c_re), pack_c(-s5_c_im)], axis=-2).astype(BF16)
    p["lam"] = jnp.stack([lr.reshape(DEPTH, S5_FLAT), li.reshape(DEPTH, S5_FLAT)], axis=1)
    p["dskB"] = s5_d[:, None, :]
    p["glu_w"] = s5_glu_w.astype(BF16)
    p["glu_b"] = s5_glu_b[:, None, :]
    return p


def _group_cfg(B, T):
    if T % 512 == 0:
        return dict(Tp=None, ssd=(1, 256, 256, 2), ml=(1, 256, 256, 2), xa=(1, 256), s5=(8, 32), tm=256)
    assert T == 4 and B % 32 == 0
    return dict(Tp=True, ssd=(8, 16, T, 1), ml=(4, 32, T, 1), xa=(4, 8), s5=(32, T), tm=B * T)


def _pad_time(h, Lp):
    return jnp.pad(h, ((0, 0), (0, Lp - h.shape[1]), (0, 0)))


def _layer(p, l, x, h, h_tm, mk, mv, lkv, states, ls, prev, g_next, last):
    B, T, _ = x.shape
    cfg = _group_cfg(B, T)
    padded = cfg["Tp"] is not None
    conv0, ssd0, sre0, sim0, c0, n0, m0 = states
    prev = prev or dict(a=None, b=None, c=None)

    nb, L, valid, nsub = cfg["ssd"]
    hin = _pad_time(h, L) if padded else h
    ya, st_a = _ssd_call(hin, p, conv0, ssd0, l, ls, prev["a"], nb=nb, L=L, valid=valid, nsub=nsub)
    nb, L, valid, nsub = cfg["ml"]
    hin = _pad_time(h, L) if padded else h
    yc, st_c = _mlstm_call(hin, p, c0, n0, m0, l, ls, prev["c"], nb=nb, L=L, valid=valid, nsub=nsub)
    nb, L = cfg["xa"]
    hin = _pad_time(h, L) if padded else h
    yd = (_xattn_rows_call if padded else _xattn_call)(hin, p["wD"], mk, mv, l, lkv, nb=nb, L=L)
    nb, Lt = cfg["s5"]
    NB = B // nb
    yb, st_b = _s5_call(h_tm, p, sre0, sim0, l, ls, prev["b"], nb=nb, Lt=Lt)
    new_states = dict(a=st_a, b=st_b, c=st_c)

    if padded:
        flat = lambda a: a[:, :T].reshape(1, B * T, D_MODEL)
        yb_flat = yb.reshape(NB, T, nb, D_MODEL).transpose(0, 2, 1, 3).reshape(B * T, D_MODEL)
        outs = _merge_call(h.reshape(1, B * T, D_MODEL), flat(ya), yb_flat, flat(yc), flat(yd),
                           x.reshape(1, B * T, D_MODEL), p["wE"], p["b_gate"], p["w_down"], p["w_out"], g_next, l,
                           tm=cfg["tm"], last=last)
        if last:
            res = (outs[0].reshape(B, T, D_MODEL), None, None)
        else:
            hn = outs[1].reshape(B, T, D_MODEL)
            hn_tm = hn.reshape(NB, nb, T, D_MODEL).transpose(0, 2, 1, 3).reshape(NB, T * nb, D_MODEL)
            res = (outs[0].reshape(B, T, D_MODEL), hn, hn_tm)
    else:
        outs = _merge_call(h, ya, yb.reshape(T, B * D_MODEL), yc, yd, x, p["wE"], p["b_gate"], p["w_down"],
                           p["w_out"], g_next, l, tm=cfg["tm"], last=last)
        if last:
            res = (outs[0], None, None)
        else:
            res = (outs[0], outs[1], outs[2].reshape(1, T * B, D_MODEL))
    return res, new_states


def _first_norm(x, g):
    B, T, _ = x.shape
    cfg = _group_cfg(B, T)
    if cfg["Tp"] is None:
        h, h_tm = _norm_call(x, g, cfg["tm"])
        return h, h_tm.reshape(1, T * B, D_MODEL)
    h, _ = _norm_call(x.reshape(1, B * T, D_MODEL), g, B * T)
    h = h.reshape(B, T, D_MODEL)
    nb = cfg["s5"][0]
    h_tm = h.reshape(B // nb, nb, T, D_MODEL).transpose(0, 2, 1, 3).reshape(B // nb, T * nb, D_MODEL)
    return h, h_tm


def kernel(x_prompt, x_sample, mem_prompt, cache_mem_k, cache_mem_v, state_ssd_conv, state_ssd, state_s5_re,
           state_s5_im, state_mlstm_c, state_mlstm_n, state_mlstm_m, norm_in, w_in, b_gate, b_igate, b_fgate,
           ssd_conv_w, ssd_conv_b, ssd_dt_bias, ssd_a_log, ssd_d, ssd_norm, s5_a_re, s5_a_im, s5_log_dt, s5_b_re,
           s5_b_im, s5_c_re, s5_c_im, s5_d, s5_glu_w, s5_glu_b, ml_norm, mem_norm, w_mem_kv, w_down, w_out,
           final_norm):
    p = _pack_params(w_in, b_gate, b_igate, b_fgate, ssd_conv_w, ssd_conv_b, ssd_dt_bias, ssd_a_log, ssd_d,
                     ssd_norm, s5_a_re, s5_a_im, s5_log_dt, s5_b_re, s5_b_im, s5_c_re, s5_c_im, s5_d, s5_glu_w,
                     s5_glu_b, ml_norm, w_down, w_out)
    Bp, Tp, _ = x_prompt.shape
    Bs, Ts, _ = x_sample.shape

    mk_p, mv_p, mk_out, mv_out = _memkv_call(mem_prompt, mem_norm[:, None, :], w_mem_kv.astype(BF16))

    zeros = lambda *s: jnp.zeros((1,) + s, F32)
    states_p = (zeros(Bp, SSD_CONV - 1, SSD_CONV_DIM), zeros(Bp, SSD_HEADS, SSD_HEAD_DIM, SSD_STATE),
                zeros(Bp * S5_GROUPS, S5_STATE), zeros(Bp * S5_GROUPS, S5_STATE),
                zeros(Bp, ML_HEADS, ML_HEAD_DIM, ML_HEAD_DIM), zeros(Bp, ML_HEADS, ML_HEAD_DIM),
                zeros(Bp, 1, LANES))
    m_pad = jnp.pad(state_mlstm_m, ((0, 0), (0, 0), (0, LANES - ML_HEADS)))[:, :, None, :]
    states_s = (state_ssd_conv, state_ssd, state_s5_re.reshape(DEPTH, Bs * S5_GROUPS, S5_STATE),
                state_s5_im.reshape(DEPTH, Bs * S5_GROUPS, S5_STATE), state_mlstm_c, state_mlstm_n, m_pad)

    g0 = norm_in[0][None, :]
    xp, (hp, hp_tm) = x_prompt, _first_norm(x_prompt, g0)
    xs, (hs, hs_tm) = x_sample, _first_norm(x_sample, g0)
    st_p = st_s = None
    for l in range(DEPTH):
        last = l == DEPTH - 1
        g_next = final_norm[None, :] if last else norm_in[l + 1][None, :]
        (xp, hp, hp_tm), st_p = _layer(p, l, xp, hp, hp_tm, mk_p, mv_p, l, states_p, 0, st_p, g_next, last)
        (xs, hs, hs_tm), st_s = _layer(p, l, xs, hs, hs_tm, cache_mem_k, cache_mem_v, l, states_s, l, st_s,
                                       g_next, last)

    def unpack(st, B):
        conv, ssd = st["a"]
        sre, sim = st["b"]
        c, n, m = st["c"]
        return (conv, ssd, sre.reshape(DEPTH, B, S5_GROUPS, S5_STATE), sim.reshape(DEPTH, B, S5_GROUPS, S5_STATE),
                c, n, m[:, :, 0, :ML_HEADS])

    return (xp, xs, mk_out, mv_out) + unpack(st_p, Bp) + unpack(st_s, Bs)
```

```python
import functools
import math

import jax
import jax.numpy as jnp
from jax import lax
from jax.experimental import pallas as pl
from jax.experimental.pallas import tpu as pltpu

F32 = jnp.float32
BF16 = jnp.bfloat16
NEG_INF = float("-inf")

D_MODEL = 1024
DEPTH = 4
BR_WIDTH = D_MODEL
SSD_HEADS = 16
SSD_HEAD_DIM = 64
SSD_GROUPS = 2
SSD_STATE = 128
SSD_CONV = 4
SSD_CONV_DIM = BR_WIDTH + 2 * SSD_GROUPS * SSD_STATE
S5_GROUP = 16
S5_GROUPS = 64
S5_STATE = 64
S5_CLUSTERS = 4
S5_CLUSTER_GROUPS = S5_GROUPS // S5_CLUSTERS
S5_FLAT = S5_GROUPS * S5_STATE
ML_HEADS = 4
ML_HEAD_DIM = 256
MEM_TOKENS = 256
XA_HEADS = 4
XA_HEAD_DIM = 256
EPS = 1e-6
LANES = 128
CONV_PAD = 8

IN_NAMES = ("z_ssd", "xbc", "dt", "u_s5", "z_s5", "q", "k", "v", "i", "f", "o", "z_ml", "q_xa", "z_xa", "gate")
IN_SIZES = (BR_WIDTH, SSD_CONV_DIM, SSD_HEADS, BR_WIDTH, BR_WIDTH, BR_WIDTH, BR_WIDTH, BR_WIDTH,
            ML_HEADS, ML_HEADS, BR_WIDTH, BR_WIDTH, BR_WIDTH, BR_WIDTH, 4 * D_MODEL)

WA_COLS = SSD_CONV_DIM + BR_WIDTH + LANES
WB_COLS = 2 * BR_WIDTH
WC_COLS = 3 * BR_WIDTH + 2 * LANES + 2 * BR_WIDTH
WD_COLS = 2 * BR_WIDTH

VMEM_LIMIT = 56 * 1024 * 1024


def _dot(a, b):
    return jnp.dot(a, b, preferred_element_type=F32)


def _dot_nt(a, b):
    return lax.dot_general(a, b, (((1,), (1,)), ((), ())), preferred_element_type=F32)


def _split3(x):
    hi = x.astype(BF16)
    r = x - hi.astype(F32)
    mid = r.astype(BF16)
    lo = (r - mid.astype(F32)).astype(BF16)
    return hi, mid, lo


def _sel_left(sel, x):
    hi, mid, lo = _split3(x)
    return _dot(sel, hi) + _dot(sel, mid) + _dot(sel, lo)


def _sel_right(x, sel):
    hi, mid, lo = _split3(x)
    return _dot(hi, sel) + _dot(mid, sel) + _dot(lo, sel)


def _softplus(x):
    return jnp.maximum(x, 0.0) + jnp.log1p(jnp.exp(-jnp.abs(x)))


def _silu(x):
    return x * jax.nn.sigmoid(x)


def _rms(x, g):
    return x * lax.rsqrt(jnp.mean(x * x, axis=-1, keepdims=True) + EPS) * g


def _seq_masks(nb, L):
    M = nb * L
    ri = lax.broadcasted_iota(jnp.int32, (M, M), 0)
    ci = lax.broadcasted_iota(jnp.int32, (M, M), 1)
    if nb > 1:
        sh = int(math.log2(L))
        same = lax.shift_right_logical(ri, sh) == lax.shift_right_logical(ci, sh)
        causal = same & (ci <= ri)
        upper = same & (ri <= ci)
    else:
        same = None
        causal = ci <= ri
        upper = ri <= ci
    return same, causal, upper


def _valid_masks(nb, L, valid):
    M = nb * L
    rv = (lax.broadcasted_iota(jnp.int32, (M, 1), 0) & (L - 1)) < valid
    cv = (lax.broadcasted_iota(jnp.int32, (1, M), 1) & (L - 1)) < valid
    return rv, cv


def _row_sel(nb, L, b, dtype):
    rows = lax.broadcasted_iota(jnp.int32, (nb * L, 1), 0)
    return (lax.shift_right_logical(rows, int(math.log2(L))) == b).astype(dtype)


def _col_pieces(n_cols, width):
    return [(c0, min(c0 + width, n_cols)) for c0 in range(0, n_cols, width)]


class _Spread:
    def __init__(self, thunks, n_slots):
        self.thunks, self.n_slots, self.calls, self.done = list(thunks), n_slots, 0, 0

    def slot(self):
        self.calls += 1
        while self.done < len(self.thunks) and self.done * self.n_slots < self.calls * len(self.thunks):
            self.thunks[self.done]()
            self.done += 1

    def flush(self):
        while self.done < len(self.thunks):
            self.thunks[self.done]()
            self.done += 1


def _proj_piece(proj_scr, slot, hb, w_ref, c0, c1):
    proj_scr[slot, :, c0:c1] = _dot(hb, w_ref[:, c0:c1])


def _next_chunk_spec(nb, L, nsub, n_steps, n_blocks):
    def index(i, c):
        nxt = jnp.minimum(i * n_steps + c + 1, n_blocks * n_steps - 1)
        return (nxt // n_steps, (nxt % n_steps) * nsub, 0)
    return pl.BlockSpec((nb, L, D_MODEL), index)


def _fill_padded_proj(proj_scr, p_ref, nb, L, valid):
    @pl.when((pl.program_id(0) == 0) & (pl.program_id(1) == 0))
    def _zero():
        proj_scr[0] = jnp.zeros(proj_scr.shape[1:], F32)

    for b in range(nb):
        proj_scr[0, b * L:b * L + valid, :] = p_ref[b]


def _proj_body(h_ref, w_ref, o_ref):
    o_ref[...] = _dot(h_ref[...], w_ref[...])


def _proj_call(h2d, w, l, tn):
    R = h2d.shape[0]
    cols = w.shape[-1]
    return pl.pallas_call(
        _proj_body,
        grid=(cols // tn,),
        in_specs=[pl.BlockSpec((R, D_MODEL), lambda j: (0, 0)),
                  pl.BlockSpec((None, D_MODEL, tn), lambda j: (l, 0, j))],
        out_specs=pl.BlockSpec((R, tn), lambda j: (0, j)),
        out_shape=jax.ShapeDtypeStruct((R, cols), F32),
        compiler_params=pltpu.CompilerParams(dimension_semantics=("arbitrary",), vmem_limit_bytes=VMEM_LIMIT),
        name="in_proj",
    )(h2d, w)


def _layer_spec(shape, l):
    return pl.BlockSpec((None,) + shape, lambda *_: (l,) + (0,) * len(shape), pipeline_mode=pl.Buffered(1))


def _state_spec(shape, l):
    return pl.BlockSpec((None,) + shape, lambda i, c: (l, i) + (0,) * (len(shape) - 1))


def _branch_call(body, *, name, grid, in_specs, args, y_spec, y_shape, st_specs, st_shapes, prev, scratch):
    n_in = len(args)
    prev = () if prev is None else tuple(prev)

    def wrapped(*refs):
        body(*refs[:n_in], *refs[n_in + len(prev):])

    outs = pl.pallas_call(
        wrapped,
        grid=grid,
        in_specs=list(in_specs) + [pl.BlockSpec(memory_space=pl.ANY)] * len(prev),
        out_specs=[y_spec] + list(st_specs),
        out_shape=[y_shape] + list(st_shapes),
        scratch_shapes=scratch,
        input_output_aliases={n_in + k: 1 + k for k in range(len(prev))},
        compiler_params=pltpu.CompilerParams(dimension_semantics=("arbitrary", "arbitrary"),
                                             vmem_limit_bytes=VMEM_LIMIT),
        name=name,
    )(*args, *prev)
    return outs[0], tuple(outs[1:])


def _norm_body(x_ref, g_ref, h_ref, htm_ref):
    hb = _rms(x_ref[0], g_ref[...]).astype(BF16)
    h_ref[0] = hb
    htm_ref[...] = hb


def _norm_call(x, g, tm):
    B, T, _ = x.shape
    return pl.pallas_call(
        _norm_body,
        grid=(B, T // tm),
        in_specs=[pl.BlockSpec((1, tm, D_MODEL), lambda b, t: (b, t, 0)),
                  pl.BlockSpec((1, D_MODEL), lambda b, t: (0, 0))],
        out_specs=[pl.BlockSpec((1, tm, D_MODEL), lambda b, t: (b, t, 0)),
                   pl.BlockSpec((tm, D_MODEL), lambda b, t: (t, b))],
        out_shape=[jax.ShapeDtypeStruct((B, T, D_MODEL), BF16),
                   jax.ShapeDtypeStruct((T, B * D_MODEL), BF16)],
        compiler_params=pltpu.CompilerParams(dimension_semantics=("arbitrary", "arbitrary")),
        name="rmsnorm_in",
    )(x, g)


def _memkv_body(mem_ref, g_ref, w_ref, mk_ref, mv_ref, mk5_ref, mv5_ref):
    hb = _rms(mem_ref[0], g_ref[0]).astype(BF16)
    kv = _dot(hb, w_ref[0])
    mk_ref[0, 0] = kv[:, :BR_WIDTH]
    mv_ref[0, 0] = kv[:, BR_WIDTH:]
    for hd in range(XA_HEADS):
        lo, hi = hd * XA_HEAD_DIM, (hd + 1) * XA_HEAD_DIM
        mk5_ref[0, 0, :, hd, :] = kv[:, lo:hi]
        mv5_ref[0, 0, :, hd, :] = kv[:, BR_WIDTH + lo:BR_WIDTH + hi]


def _memkv_call(mem, g, w_kv):
    B = mem.shape[0]
    out = jax.ShapeDtypeStruct((DEPTH, B, MEM_TOKENS, BR_WIDTH), F32)
    out5 = jax.ShapeDtypeStruct((DEPTH, B, MEM_TOKENS, XA_HEADS, XA_HEAD_DIM), F32)
    spec5 = pl.BlockSpec((1, 1, MEM_TOKENS, XA_HEADS, XA_HEAD_DIM), lambda l, b: (l, b, 0, 0, 0))
    return pl.pallas_call(
        _memkv_body,
        grid=(DEPTH, B),
        in_specs=[pl.BlockSpec((1, MEM_TOKENS, D_MODEL), lambda l, b: (b, 0, 0)),
                  pl.BlockSpec((1, 1, D_MODEL), lambda l, b: (l, 0, 0)),
                  pl.BlockSpec((1, D_MODEL, 2 * BR_WIDTH), lambda l, b: (l, 0, 0))],
        out_specs=[pl.BlockSpec((1, 1, MEM_TOKENS, BR_WIDTH), lambda l, b: (l, b, 0, 0)),
                   pl.BlockSpec((1, 1, MEM_TOKENS, BR_WIDTH), lambda l, b: (l, b, 0, 0)), spec5, spec5],
        out_shape=[out, out, out5, out5],
        compiler_params=pltpu.CompilerParams(dimension_semantics=("arbitrary", "arbitrary"),
                                             vmem_limit_bytes=VMEM_LIMIT),
        name="mem_kv",
    )(mem, g, w_kv)


def _ssd_body(h_ref, hn_ref, w_ref, wst_ref, prow_ref, pcol_ref, cw_ref, cb_ref, dsk_ref, nrm_ref, buf0_ref,
              s0_ref, y_ref, bufo_ref, so_ref, xp_scr, yin_scr, yst_scr, proj_scr, *, nb, L, valid, nsub):
    M = nb * L
    pipelined = nsub % 2 == 0
    chunk = lambda ref, s: ref[:, s * L:(s + 1) * L, :].reshape(M, D_MODEL)

    @pl.when(pl.program_id(1) == 0)
    def _init():
        so_ref[...] = s0_ref[...]
        xp_scr[:, CONV_PAD - 3:CONV_PAD, :] = buf0_ref[...]

    if pipelined:
        @pl.when((pl.program_id(0) == 0) & (pl.program_id(1) == 0))
        def _prologue():
            proj_scr[0] = _dot(chunk(h_ref, 0), w_ref[...])
    else:
        assert nsub == 1
        _fill_padded_proj(proj_scr, w_ref, nb, L, valid)

    for s in range(nsub):
        hb_next = chunk(h_ref, s + 1) if s + 1 < nsub else (hn_ref[...].reshape(M, D_MODEL) if pipelined else None)
        thunks = [] if hb_next is None else [
            functools.partial(_proj_piece, proj_scr, (s + 1) % 2, hb_next, w_ref, c0, c1)
            for c0, c1 in _col_pieces(WA_COLS, 512)]
        _ssd_chunk(chunk(h_ref, s), proj_scr.at[s % 2], s, _Spread(thunks, SSD_HEADS * (1 + nb)),
                   wst_ref, prow_ref, pcol_ref, cw_ref, cb_ref, dsk_ref, nrm_ref,
                   y_ref, bufo_ref, so_ref, xp_scr, yin_scr, yst_scr, nb=nb, L=L, valid=valid)


def _ssd_chunk(hb, proj, s, spread, wst_ref, prow_ref, pcol_ref, cw_ref, cb_ref, dsk_ref, nrm_ref,
               y_ref, bufo_ref, so_ref, xp_scr, yin_scr, yst_scr, *, nb, L, valid):
    M = nb * L
    hg = SSD_HEADS // SSD_GROUPS

    xp_scr[:, CONV_PAD:CONV_PAD + L, :] = proj[:, :SSD_CONV_DIM].reshape(nb, L, SSD_CONV_DIM)
    conv = cb_ref[...][None]
    for k in range(SSD_CONV):
        o = CONV_PAD - 3 + k
        conv = conv + xp_scr[:, o:o + L, :] * cw_ref[k:k + 1, :][None]
    nbuf = xp_scr[:, CONV_PAD - 3 + valid:CONV_PAD + valid, :]
    xp_scr[:, CONV_PAD - 3:CONV_PAD, :] = nbuf
    bufo_ref[...] = nbuf
    xbc = _silu(conv).reshape(M, SSD_CONV_DIM)
    xs = xbc[:, :BR_WIDTH]
    bm = xbc[:, BR_WIDTH:BR_WIDTH + SSD_GROUPS * SSD_STATE].astype(BF16)
    cm = xbc[:, BR_WIDTH + SSD_GROUPS * SSD_STATE:].astype(BF16)
    z = proj[:, SSD_CONV_DIM:SSD_CONV_DIM + BR_WIDTH]

    same, causal, upper = _seq_masks(nb, L)
    tril_b = causal.astype(BF16)
    triu_b = upper.astype(BF16)
    same_b = jnp.ones((M, M), BF16) if same is None else same.astype(BF16)

    dt = _softplus(proj[:, SSD_CONV_DIM + BR_WIDTH:] + prow_ref[0:1, :])
    dtT = _softplus(_dot_nt(wst_ref[...], hb) + pcol_ref[:, 0:1])
    dt = jnp.where(lax.broadcasted_iota(jnp.int32, (1, LANES), 1) < SSD_HEADS, dt, 0.0)
    if valid < L:
        rv, cv = _valid_masks(nb, L, valid)
        dt = jnp.where(rv, dt, 0.0)
        dtT = jnp.where(cv, dtT, 0.0)
    la = dt * (-jnp.exp(prow_ref[1:2, :]))
    laT = dtT * (-jnp.exp(pcol_ref[:, 1:2]))
    acs = _sel_left(tril_b, la)
    tot = _sel_left(same_b, la)
    acsT = _sel_right(laT, triu_b)

    totT = _sel_right(laT, same_b)
    tailT = jnp.exp(totT - acsT)
    eend = jnp.exp(tot)
    expand = (lax.shift_right_logical(lax.broadcasted_iota(jnp.int32, (LANES, BR_WIDTH), 1), 6)
              == lax.broadcasted_iota(jnp.int32, (LANES, BR_WIDTH), 0)).astype(BF16)
    eacs_x = _sel_right(jnp.exp(acs), expand)

    xsT = xs.T
    wT = dtT * tailT

    for g in range(SSD_GROUPS):
        cm_g = cm[:, g * SSD_STATE:(g + 1) * SSD_STATE]
        bm_g = bm[:, g * SSD_STATE:(g + 1) * SSD_STATE]
        cb = _dot_nt(cm_g, bm_g)
        ats = []
        for hh in range(hg):
            h = g * hg + hh
            lo, hi = h * SSD_HEAD_DIM, (h + 1) * SSD_HEAD_DIM
            seg = acs[:, h:h + 1] - acsT[h:h + 1, :]
            decay = jnp.exp(jnp.where(causal, seg, NEG_INF))
            xdtT = (xsT[lo:hi, :] * dtT[h:h + 1, :]).astype(BF16)
            ats.append((xsT[lo:hi, :] * wT[h:h + 1, :]).astype(BF16))
            yin_scr[:, lo:hi] = _dot_nt((cb * decay).astype(BF16), xdtT)
            spread.slot()
        at_g = jnp.concatenate(ats, axis=0)
        glo, ghi = g * hg, (g + 1) * hg
        for b in range(nb):
            r0, r1 = b * L, (b + 1) * L
            bm_b = bm_g if nb == 1 else bm_g * _row_sel(nb, L, b, BF16)
            s_old = so_ref[b, glo:ghi].reshape(hg * SSD_HEAD_DIM, SSD_STATE)
            yst_scr[r0:r1, glo * SSD_HEAD_DIM:ghi * SSD_HEAD_DIM] = _dot_nt(cm_g[r0:r1, :], s_old.astype(BF16))
            kept = jnp.concatenate(
                [eend[r0:r0 + 1, glo + hh:glo + hh + 1] * s_old[hh * SSD_HEAD_DIM:(hh + 1) * SSD_HEAD_DIM]
                 for hh in range(hg)], axis=0)
            so_ref[b, glo:ghi] = (kept + _dot(at_g, bm_b)).reshape(hg, SSD_HEAD_DIM, SSD_STATE)
            for _ in range(hg):
                spread.slot()
    spread.flush()

    y = yin_scr[...] + eacs_x * yst_scr[...] + dsk_ref[...] * xs
    y_ref[:, s * L:(s + 1) * L, :] = _rms(y * _silu(z), nrm_ref[...]).reshape(nb, L, BR_WIDTH).astype(BF16)


def _ssd_call(h, p, buf0, s0, l, ls, prev, *, nb, L, valid, nsub, proj=None):
    B, T, _ = h.shape
    wspec = lambda shape: _layer_spec(shape, l)
    conv_blk = (nb, SSD_CONV - 1, SSD_CONV_DIM)
    ssd_blk = (nb, SSD_HEADS, SSD_HEAD_DIM, SSD_STATE)
    w_arg, w_spec = ((p["wA"], wspec((D_MODEL, WA_COLS))) if proj is None else
                     (proj, pl.BlockSpec((nb, valid, WA_COLS), lambda i, c: (i, 0, 0))))
    return _branch_call(
        functools.partial(_ssd_body, nb=nb, L=L, valid=valid, nsub=nsub),
        name="ssd_branch",
        grid=(B // nb, T // (nsub * L)),
        in_specs=[pl.BlockSpec((nb, nsub * L, D_MODEL), lambda i, c: (i, c, 0)),
                  _next_chunk_spec(nb, L, nsub, T // (nsub * L), B // nb),
                  w_spec, wspec((SSD_HEADS, D_MODEL)),
                  wspec((8, LANES)), wspec((SSD_HEADS, 2)),
                  wspec((SSD_CONV, SSD_CONV_DIM)), wspec((1, SSD_CONV_DIM)),
                  wspec((1, BR_WIDTH)), wspec((1, BR_WIDTH)),
                  _state_spec(conv_blk, ls), _state_spec(ssd_blk, ls)],
        args=(h, h, w_arg, p["wstA"], p["prowA"], p["pcolA"], p["conv_w"], p["conv_b"], p["dskA"], p["nrmA"],
              buf0, s0),
        y_spec=pl.BlockSpec((nb, nsub * L, BR_WIDTH), lambda i, c: (i, c, 0)),
        y_shape=jax.ShapeDtypeStruct((B, T, BR_WIDTH), BF16),
        st_specs=[_state_spec(conv_blk, l), _state_spec(ssd_blk, l)],
        st_shapes=[jax.ShapeDtypeStruct((DEPTH, B) + conv_blk[1:], F32),
                   jax.ShapeDtypeStruct((DEPTH, B) + ssd_blk[1:], F32)],
        prev=prev,
        scratch=[pltpu.VMEM((nb, L + CONV_PAD, SSD_CONV_DIM), F32),
                 pltpu.VMEM((nb * L, BR_WIDTH), F32),
                 pltpu.VMEM((nb * L, BR_WIDTH), F32),
                 pltpu.VMEM((2, nb * L, WA_COLS), F32)])


S5_SCAN_LANES = 512


def _s5_body(h_ref, w_ref, bc_ref, cc_ref, lam_ref, dsk_ref, gw_ref, gb_ref, sre0_ref, sim0_ref,
             y_ref, sre_ref, sim_ref, hs_scr, st_scr, *, nb, Lt):
    half = S5_CLUSTER_GROUPS * S5_STATE

    @pl.when(pl.program_id(1) == 0)
    def _load_state():
        for g in range(S5_GROUPS):
            st_scr[0, :, g * S5_STATE:(g + 1) * S5_STATE] = sre0_ref[pl.ds(g, nb, stride=S5_GROUPS), :]
            st_scr[1, :, g * S5_STATE:(g + 1) * S5_STATE] = sim0_ref[pl.ds(g, nb, stride=S5_GROUPS), :]

    hb = h_ref[0]
    proj = _dot(hb, w_ref[...])
    u = proj[:, :BR_WIDTH]
    z = proj[:, BR_WIDTH:]
    ub = u.astype(BF16)
    gl = S5_CLUSTER_GROUPS * S5_GROUP
    ys = []
    for j in range(S5_CLUSTERS):
        hs_scr[:, 2 * half * j:2 * half * (j + 1)] = _dot(ub[:, gl * j:gl * (j + 1)], bc_ref[j])
        for q in range(half // S5_SCAN_LANES):
            cr = 2 * half * j + S5_SCAN_LANES * q
            ci = cr + half
            sc = half * j + S5_SCAN_LANES * q
            lr = jnp.broadcast_to(lam_ref[0:1, sc:sc + S5_SCAN_LANES], (8, S5_SCAN_LANES))
            li = jnp.broadcast_to(lam_ref[1:2, sc:sc + S5_SCAN_LANES], (8, S5_SCAN_LANES))
            for sg in range(nb // 8):
                hr = st_scr[0, 8 * sg:8 * sg + 8, sc:sc + S5_SCAN_LANES]
                hi = st_scr[1, 8 * sg:8 * sg + 8, sc:sc + S5_SCAN_LANES]
                for t in range(Lt):
                    r0 = t * nb + 8 * sg
                    nr = lr * hr - li * hi + hs_scr[r0:r0 + 8, cr:cr + S5_SCAN_LANES]
                    ni = lr * hi + li * hr + hs_scr[r0:r0 + 8, ci:ci + S5_SCAN_LANES]
                    hs_scr[r0:r0 + 8, cr:cr + S5_SCAN_LANES] = nr
                    hs_scr[r0:r0 + 8, ci:ci + S5_SCAN_LANES] = ni
                    hr, hi = nr, ni
                st_scr[0, 8 * sg:8 * sg + 8, sc:sc + S5_SCAN_LANES] = hr
                st_scr[1, 8 * sg:8 * sg + 8, sc:sc + S5_SCAN_LANES] = hi
        ys.append(_dot(hs_scr[:, 2 * half * j:2 * half * (j + 1)].astype(BF16), cc_ref[j]))

    y = jnp.concatenate(ys, axis=1) + dsk_ref[...] * u
    yb = jax.nn.gelu(y)
    glu = jax.nn.sigmoid(_dot(yb.astype(BF16), gw_ref[...]) + gb_ref[...])
    y_ref[0] = (yb * glu * _silu(z)).astype(BF16)

    @pl.when(pl.program_id(1) == pl.num_programs(1) - 1)
    def _store_state():
        for g in range(S5_GROUPS):
            sre_ref[pl.ds(g, nb, stride=S5_GROUPS), :] = st_scr[0, :, g * S5_STATE:(g + 1) * S5_STATE]
            sim_ref[pl.ds(g, nb, stride=S5_GROUPS), :] = st_scr[1, :, g * S5_STATE:(g + 1) * S5_STATE]


def _s5_call(h_tm, p, sre0, sim0, l, ls, prev, *, nb, Lt):
    NB, R, _ = h_tm.shape
    T = R // nb
    wspec = lambda shape: _layer_spec(shape, l)
    st_blk = (nb * S5_GROUPS, S5_STATE)
    st_shape = jax.ShapeDtypeStruct((DEPTH, NB * nb * S5_GROUPS, S5_STATE), F32)
    return _branch_call(
        functools.partial(_s5_body, nb=nb, Lt=Lt),
        name="s5_branch",
        grid=(NB, T // Lt),
        in_specs=[pl.BlockSpec((1, nb * Lt, D_MODEL), lambda i, c: (i, c, 0)),
                  wspec((D_MODEL, WB_COLS)),
                  wspec((S5_CLUSTERS, S5_CLUSTER_GROUPS * S5_GROUP, 2 * S5_CLUSTER_GROUPS * S5_STATE)),
                  wspec((S5_CLUSTERS, 2 * S5_CLUSTER_GROUPS * S5_STATE, S5_CLUSTER_GROUPS * S5_GROUP)),
                  wspec((2, S5_FLAT)), wspec((1, BR_WIDTH)),
                  wspec((BR_WIDTH, BR_WIDTH)), wspec((1, BR_WIDTH)),
                  _state_spec(st_blk, ls), _state_spec(st_blk, ls)],
        args=(h_tm, p["wB"], p["bc"], p["cc"], p["lam"], p["dskB"], p["glu_w"], p["glu_b"], sre0, sim0),
        y_spec=pl.BlockSpec((1, nb * Lt, BR_WIDTH), lambda i, c: (i, c, 0)),
        y_shape=jax.ShapeDtypeStruct((NB, R, BR_WIDTH), BF16),
        st_specs=[_state_spec(st_blk, l), _state_spec(st_blk, l)],
        st_shapes=[st_shape, st_shape],
        prev=prev,
        scratch=[pltpu.VMEM((nb * Lt, 2 * S5_FLAT), F32), pltpu.VMEM((2, nb, S5_FLAT), F32)])


def _mlstm_body(h_ref, hn_ref, w_ref, wst_ref, prow_ref, pcol_ref, nrm_ref, c0_ref, n0_ref, m0_ref,
                y_ref, co_ref, no_ref, mo_ref, num_scr, proj_scr, *, nb, L, valid, nsub):
    M = nb * L
    pipelined = nsub % 2 == 0
    chunk = lambda ref, s: ref[:, s * L:(s + 1) * L, :].reshape(M, D_MODEL)

    @pl.when(pl.program_id(1) == 0)
    def _init():
        co_ref[...] = c0_ref[...]
        no_ref[...] = n0_ref[...]
        mo_ref[...] = m0_ref[...]

    if pipelined:
        @pl.when((pl.program_id(0) == 0) & (pl.program_id(1) == 0))
        def _prologue():
            proj_scr[0] = _dot(chunk(h_ref, 0), w_ref[...])
    else:
        assert nsub == 1
        proj_scr[0] = _dot(chunk(h_ref, 0), w_ref[...])

    for s in range(nsub):
        hb_next = chunk(h_ref, s + 1) if s + 1 < nsub else (hn_ref[...].reshape(M, D_MODEL) if pipelined else None)
        thunks = [] if hb_next is None else [
            functools.partial(_proj_piece, proj_scr, (s + 1) % 2, hb_next, w_ref, c0, c1)
            for c0, c1 in _col_pieces(WC_COLS, 768)]
        _mlstm_chunk(chunk(h_ref, s), proj_scr.at[s % 2], s, _Spread(thunks, ML_HEADS * (1 + nb)),
                     wst_ref, prow_ref, pcol_ref, nrm_ref, y_ref, co_ref, no_ref, mo_ref, num_scr,
                     nb=nb, L=L, valid=valid)


def _mlstm_chunk(hb, proj, s, spread, wst_ref, prow_ref, pcol_ref, nrm_ref, y_ref, co_ref, no_ref, mo_ref,
                 num_scr, *, nb, L, valid):
    M = nb * L
    W = BR_WIDTH
    q = proj[:, :W]
    k = proj[:, W:2 * W] * (ML_HEAD_DIM ** -0.5)
    v = proj[:, 2 * W:3 * W]
    ig = proj[:, 3 * W:3 * W + LANES] + prow_ref[0:1, :]
    logf = -_softplus(-(proj[:, 3 * W + LANES:3 * W + 2 * LANES] + prow_ref[1:2, :]))
    o = proj[:, 3 * W + 2 * LANES:4 * W + 2 * LANES]
    z = proj[:, 4 * W + 2 * LANES:]
    smallT = _dot_nt(wst_ref[...], hb) + pcol_ref[...]
    igT = smallT[0:8, :]
    logfT = -_softplus(-smallT[8:16, :])
    if valid < L:
        rv, cv = _valid_masks(nb, L, valid)
        ig = jnp.where(rv, ig, NEG_INF)
        logf = jnp.where(rv, logf, 0.0)
        igT = jnp.where(cv, igT, NEG_INF)
        logfT = jnp.where(cv, logfT, 0.0)

    same, causal, upper = _seq_masks(nb, L)
    tril_b = causal.astype(BF16)
    triu_b = upper.astype(BF16)
    same_b = jnp.ones((M, M), BF16) if same is None else same.astype(BF16)
    bcum = _sel_left(tril_b, logf)
    tot = _sel_left(same_b, logf)
    bcumT = _sel_right(logfT, triu_b)
    totT = _sel_right(logfT, same_b)
    m_rows = jnp.broadcast_to(mo_ref[...], (nb, L, LANES)).reshape(M, LANES)
    gcar = bcum + m_rows
    g_end = tot + m_rows
    d_end = tot - bcum + ig
    d_endT = totT - bcumT + igT

    qb = q.astype(BF16)
    kb = k.astype(BF16)
    vb = v.astype(BF16)
    lane = lax.broadcasted_iota(jnp.int32, (M, LANES), 1)
    m_new_all = jnp.zeros((M, LANES), F32)
    per_head = []
    wk_parts = []
    for hd in range(ML_HEADS):
        lo, hi = hd * ML_HEAD_DIM, (hd + 1) * ML_HEAD_DIM
        dmat = jnp.where(causal, bcum[:, hd:hd + 1] - bcumT[hd:hd + 1, :] + igT[hd:hd + 1, :], NEG_INF)
        m_l = jnp.maximum(gcar[:, hd:hd + 1], jnp.max(dmat, axis=1, keepdims=True))
        w_inter = jnp.exp(gcar[:, hd:hd + 1] - m_l)
        qk = _dot_nt(qb[:, lo:hi], kb[:, lo:hi]) * jnp.exp(dmat - m_l)
        num_scr[:, lo:hi] = _dot(qk.astype(BF16), vb[:, lo:hi])
        den_intra = jnp.sum(qk, axis=1, keepdims=True)
        d_row = d_endT[hd:hd + 1, :] if same is None else jnp.where(same, d_endT[hd:hd + 1, :], NEG_INF)
        m_new = jnp.maximum(g_end[:, hd:hd + 1], jnp.max(d_row, axis=1, keepdims=True))
        w_end = jnp.exp(d_end[:, hd:hd + 1] - m_new)
        dec = jnp.exp(g_end[:, hd:hd + 1] - m_new)
        wk_parts.append(w_end * k[:, lo:hi])
        m_new_all = jnp.where(lane == hd, m_new, m_new_all)
        per_head.append((m_l, w_inter, den_intra, dec))
        spread.slot()
    wk = jnp.concatenate(wk_parts, axis=1)
    kt = wk.T.astype(BF16)

    for b in range(nb):
        r0, r1 = b * L, (b + 1) * L
        sel = None if nb == 1 else _row_sel(nb, L, b, BF16)
        for hd in range(ML_HEADS):
            lo, hi = hd * ML_HEAD_DIM, (hd + 1) * ML_HEAD_DIM
            m_l, w_inter, den_intra, dec = per_head[hd]
            c_old = co_ref[b, hd]
            n_old = no_ref[b, hd:hd + 1, :]
            q_r = q[r0:r1, lo:hi]
            wi = w_inter[r0:r1]
            num = num_scr[r0:r1, lo:hi] + wi * _dot(qb[r0:r1, lo:hi], c_old.astype(BF16))
            den = den_intra[r0:r1] + wi * jnp.sum(q_r * n_old, axis=1, keepdims=True)
            hout = num / jnp.maximum(jnp.abs(den), jnp.exp(-m_l[r0:r1]))
            ho = jax.nn.sigmoid(o[r0:r1, lo:hi]) * hout
            yv = _rms(ho, nrm_ref[:, lo:hi]) * _silu(z[r0:r1, lo:hi])
            y_ref[b, s * L:(s + 1) * L, lo:hi] = yv.astype(BF16)
            dec_b = dec[r0:r0 + 1]
            vb_b = vb[:, lo:hi] if sel is None else vb[:, lo:hi] * sel
            co_ref[b, hd] = dec_b * c_old + _dot(kt[lo:hi, :], vb_b)
            no_ref[b, hd:hd + 1, :] = dec_b * n_old + jnp.sum(wk[r0:r1, lo:hi], axis=0, keepdims=True)
            spread.slot()
        mo_ref[b] = m_new_all[r0:r0 + 1, :]
    spread.flush()


def _mlstm_call(h, p, c0, n0, m0, l, ls, prev, *, nb, L, valid, nsub):
    B, T, _ = h.shape
    wspec = lambda shape: _layer_spec(shape, l)
    w_arg, w_spec = p["wC"], wspec((D_MODEL, WC_COLS))
    c_blk = (nb, ML_HEADS, ML_HEAD_DIM, ML_HEAD_DIM)
    n_blk = (nb, ML_HEADS, ML_HEAD_DIM)
    m_blk = (nb, 1, LANES)
    return _branch_call(
        functools.partial(_mlstm_body, nb=nb, L=L, valid=valid, nsub=nsub),
        name="mlstm_branch",
        grid=(B // nb, T // (nsub * L)),
        in_specs=[pl.BlockSpec((nb, nsub * L, D_MODEL), lambda i, c: (i, c, 0)),
                  _next_chunk_spec(nb, L, nsub, T // (nsub * L), B // nb),
                  w_spec, wspec((16, D_MODEL)),
                  wspec((8, LANES)), wspec((16, 1)), wspec((1, BR_WIDTH)),
                  _state_spec(c_blk, ls), _state_spec(n_blk, ls), _state_spec(m_blk, ls)],
        args=(h, h, w_arg, p["wstC"], p["prowC"], p["pcolC"], p["nrmC"], c0, n0, m0),
        y_spec=pl.BlockSpec((nb, nsub * L, BR_WIDTH), lambda i, c: (i, c, 0)),
        y_shape=jax.ShapeDtypeStruct((B, T, BR_WIDTH), BF16),
        st_specs=[_state_spec(c_blk, l), _state_spec(n_blk, l), _state_spec(m_blk, l)],
        st_shapes=[jax.ShapeDtypeStruct((DEPTH, B) + c_blk[1:], F32),
                   jax.ShapeDtypeStruct((DEPTH, B) + n_blk[1:], F32),
                   jax.ShapeDtypeStruct((DEPTH, B) + m_blk[1:], F32)],
        prev=prev,
        scratch=[pltpu.VMEM((nb * L, BR_WIDTH), F32), pltpu.VMEM((2, nb * L, WC_COLS), F32)])


def _xattn_body(h_ref, w_ref, mk_ref, mv_ref, y_ref, *, nb, L):
    M = nb * L
    hb = h_ref[...].reshape(M, D_MODEL)
    proj = _dot(hb, w_ref[...])
    qb = proj[:, :BR_WIDTH].astype(BF16)
    z = proj[:, BR_WIDTH:]
    for b in range(nb):
        r0, r1 = b * L, (b + 1) * L
        for hd in range(XA_HEADS):
            lo, hi = hd * XA_HEAD_DIM, (hd + 1) * XA_HEAD_DIM
            s = _dot_nt(qb[r0:r1, lo:hi], mk_ref[b, :, lo:hi].astype(BF16)) * (XA_HEAD_DIM ** -0.5)
            e = jnp.exp(s - jnp.max(s, axis=1, keepdims=True))
            p = e / jnp.sum(e, axis=1, keepdims=True)
            a = _dot(p.astype(BF16), mv_ref[b, :, lo:hi].astype(BF16))
            y_ref[b, :, lo:hi] = (a * _silu(z[r0:r1, lo:hi])).astype(BF16)


def _xattn_rows_body(p_ref, mk_ref, mv_ref, y_ref, pj_scr, *, nb, L, valid):
    R = MEM_TOKENS * XA_HEADS
    Q = XA_HEADS * L

    @pl.when(pl.program_id(0) == 0)
    def _zero():
        pj_scr[...] = jnp.zeros(pj_scr.shape, F32)

    for b in range(nb):
        pj_scr[b * L:b * L + valid, :] = p_ref[b]
    q = pj_scr[:, :BR_WIDTH]
    z = pj_scr[:, BR_WIDTH:]
    row_head = lax.shift_right_logical(lax.broadcasted_iota(jnp.int32, (Q, R), 0), int(math.log2(L)))
    col_head = lax.broadcasted_iota(jnp.int32, (Q, R), 1) & (XA_HEADS - 1)
    own = row_head == col_head
    for b in range(nb):
        r0, r1 = b * L, (b + 1) * L
        qall = jnp.concatenate([q[r0:r1, hd * XA_HEAD_DIM:(hd + 1) * XA_HEAD_DIM] for hd in range(XA_HEADS)],
                               axis=0).astype(BF16)
        mk2 = mk_ref[b].reshape(R, XA_HEAD_DIM).astype(BF16)
        mv2 = mv_ref[b].reshape(R, XA_HEAD_DIM).astype(BF16)
        s = jnp.where(own, _dot_nt(qall, mk2) * (XA_HEAD_DIM ** -0.5), -1e30)
        e = jnp.exp(s - jnp.max(s, axis=1, keepdims=True))
        pr = e / jnp.sum(e, axis=1, keepdims=True)
        a = _dot(pr.astype(BF16), mv2)
        for hd in range(XA_HEADS):
            lo, hi = hd * XA_HEAD_DIM, (hd + 1) * XA_HEAD_DIM
            yv = a[hd * L:(hd + 1) * L, :] * _silu(z[r0:r1, lo:hi])
            y_ref[b, :, lo:hi] = yv[:valid].astype(BF16)


def _xattn_rows_call(proj, mk, mv, lkv, *, nb, L):
    B, T, _ = proj.shape
    assert T <= L
    kv_spec = pl.BlockSpec((None, nb, MEM_TOKENS, XA_HEADS, XA_HEAD_DIM), lambda i: (lkv, i, 0, 0, 0))
    return pl.pallas_call(
        functools.partial(_xattn_rows_body, nb=nb, L=L, valid=T),
        grid=(B // nb,),
        in_specs=[pl.BlockSpec((nb, T, WD_COLS), lambda i: (i, 0, 0)), kv_spec, kv_spec],
        out_specs=pl.BlockSpec((nb, T, BR_WIDTH), lambda i: (i, 0, 0)),
        out_shape=jax.ShapeDtypeStruct((B, T, BR_WIDTH), BF16),
        scratch_shapes=[pltpu.VMEM((nb * L, WD_COLS), F32)],
        compiler_params=pltpu.CompilerParams(dimension_semantics=("arbitrary",), vmem_limit_bytes=VMEM_LIMIT),
        name="xattn_rows",
    )(proj, mk, mv)


def _xattn_call(h, wD, mk, mv, l, lkv, *, nb, L):
    B, T, _ = h.shape
    grid = (B // nb, T // L)
    kv_spec = pl.BlockSpec((None, nb, MEM_TOKENS, BR_WIDTH), lambda i, c: (lkv, i, 0, 0))
    body = functools.partial(_xattn_body, nb=nb, L=L)
    return pl.pallas_call(
        body,
        grid=grid,
        in_specs=[pl.BlockSpec((nb, L, D_MODEL), lambda i, c: (i, c, 0)),
                  _layer_spec((D_MODEL, WD_COLS), l),
                  kv_spec, kv_spec],
        out_specs=pl.BlockSpec((nb, L, BR_WIDTH), lambda i, c: (i, c, 0)),
        out_shape=jax.ShapeDtypeStruct((B, T, BR_WIDTH), BF16),
        compiler_params=pltpu.CompilerParams(dimension_semantics=("arbitrary", "arbitrary"),
                                             vmem_limit_bytes=VMEM_LIMIT),
        name="xattn_branch",
    )(h, wD, mk, mv)


def _merge_body(h_ref, ya_ref, yb_ref, yc_ref, yd_ref, x_ref, wg_ref, bg_ref, wd_ref, wo_ref, gn_ref, *outs, last):
    h = h_ref[0]
    ys = (ya_ref[0], yb_ref[...], yc_ref[0], yd_ref[0])
    merged = None
    for kbr in range(4):
        lo, hi = kbr * D_MODEL, (kbr + 1) * D_MODEL
        gate = jax.nn.sigmoid(_dot(h, wg_ref[:, lo:hi]) + bg_ref[:, lo:hi])
        term = gate * _dot(ys[kbr], wd_ref[kbr])
        merged = term if merged is None else merged + term
    xn = x_ref[0] + _dot(merged.astype(BF16), wo_ref[...])
    if last:
        outs[0][0] = _rms(xn, gn_ref[...])
    else:
        outs[0][0] = xn
        hb = _rms(xn, gn_ref[...]).astype(BF16)
        outs[1][0] = hb
        outs[2][...] = hb


def _merge_call(h, ya, yb_tm, yc, yd, x, wE, b_gate, w_down, w_out, g_next, l, *, tm, last):
    B, T, _ = x.shape
    row = pl.BlockSpec((1, tm, D_MODEL), lambda b, t: (b, t, 0))
    tmaj = pl.BlockSpec((tm, D_MODEL), lambda b, t: (t, b))
    wspec = lambda shape: _layer_spec(shape, l)
    if last:
        out_specs = [row]
        out_shape = [jax.ShapeDtypeStruct((B, T, D_MODEL), F32)]
    else:
        out_specs = [row, row, tmaj]
        out_shape = [jax.ShapeDtypeStruct((B, T, D_MODEL), F32),
                     jax.ShapeDtypeStruct((B, T, D_MODEL), BF16),
                     jax.ShapeDtypeStruct((T, B * D_MODEL), BF16)]
    return pl.pallas_call(
        functools.partial(_merge_body, last=last),
        grid=(B, T // tm),
        in_specs=[row, row, tmaj, row, row, row,
                  wspec((D_MODEL, 4 * D_MODEL)), wspec((1, 4 * D_MODEL)),
                  wspec((4, BR_WIDTH, D_MODEL)), wspec((D_MODEL, D_MODEL)),
                  pl.BlockSpec((1, D_MODEL), lambda b, t: (0, 0))],
        out_specs=out_specs,
        out_shape=out_shape,
        compiler_params=pltpu.CompilerParams(dimension_semantics=("arbitrary", "arbitrary"),
                                             vmem_limit_bytes=VMEM_LIMIT),
        name="merge_out",
    )(h, ya, yb_tm, yc, yd, x, wE, b_gate, w_down, w_out, g_next)


def _pack_params(w_in, b_gate, b_igate, b_fgate, ssd_conv_w, ssd_conv_b, ssd_dt_bias, ssd_a_log, ssd_d, ssd_norm,
                 s5_a_re, s5_a_im, s5_log_dt, s5_b_re, s5_b_im, s5_c_re, s5_c_im, s5_d, s5_glu_w, s5_glu_b,
                 ml_norm, w_down, w_out):
    offs = {}
    acc = 0
    for name, size in zip(IN_NAMES, IN_SIZES):
        offs[name] = (acc, acc + size)
        acc += size
    col = lambda name: w_in[:, :, offs[name][0]:offs[name][1]]
    padl = lambda a: jnp.pad(a, ((0, 0), (0, 0), (0, LANES - a.shape[-1])))
    p = {}
    p["wA"] = jnp.concatenate([col("xbc"), col("z_ssd"), padl(col("dt"))], axis=-1).astype(BF16)
    p["wstA"] = jnp.swapaxes(col("dt"), 1, 2).astype(BF16)
    p["wB"] = jnp.concatenate([col("u_s5"), col("z_s5")], axis=-1).astype(BF16)
    p["wC"] = jnp.concatenate([col("q"), col("k"), col("v"), padl(col("i")), padl(col("f")), col("o"),
                               col("z_ml")], axis=-1).astype(BF16)
    zrow = jnp.zeros((DEPTH, 4, D_MODEL), F32)
    p["wstC"] = jnp.concatenate([jnp.swapaxes(col("i"), 1, 2), zrow, jnp.swapaxes(col("f"), 1, 2), zrow],
                                axis=1).astype(BF16)
    p["wD"] = jnp.concatenate([col("q_xa"), col("z_xa")], axis=-1).astype(BF16)
    p["wE"] = col("gate").astype(BF16)
    p["b_gate"] = b_gate[:, None, :]
    p["w_down"] = w_down.astype(BF16)
    p["w_out"] = w_out.astype(BF16)

    padv = lambda a: jnp.pad(a, ((0, 0), (0, LANES - a.shape[-1])))
    zl = jnp.zeros((DEPTH, 6, LANES), F32)
    p["prowA"] = jnp.concatenate([padv(ssd_dt_bias)[:, None], padv(ssd_a_log)[:, None], zl], axis=1)
    p["pcolA"] = jnp.stack([ssd_dt_bias, ssd_a_log], axis=-1)
    p["conv_w"] = ssd_conv_w
    p["conv_b"] = ssd_conv_b[:, None, :]
    p["dskA"] = jnp.repeat(ssd_d, SSD_HEAD_DIM, axis=-1)[:, None, :]
    p["nrmA"] = ssd_norm[:, None, :]

    p["prowC"] = jnp.concatenate([padv(b_igate)[:, None], padv(b_fgate)[:, None], zl], axis=1)
    z4 = jnp.zeros((DEPTH, 4), F32)
    p["pcolC"] = jnp.concatenate([b_igate, z4, b_fgate, z4], axis=1)[:, :, None]
    p["nrmC"] = ml_norm.reshape(DEPTH, 1, BR_WIDTH)

    dt = jnp.exp(s5_log_dt)[:, :, None]
    mag = jnp.exp(s5_a_re * dt)
    lr = mag * jnp.cos(s5_a_im * dt)
    li = mag * jnp.sin(s5_a_im * dt)
    den = s5_a_re * s5_a_re + s5_a_im * s5_a_im
    cr = ((lr - 1.0) * s5_a_re + li * s5_a_im) / den
    ci = (li * s5_a_re - (lr - 1.0) * s5_a_im) / den
    bb_re = cr[..., None] * s5_b_re - ci[..., None] * s5_b_im
    bb_im = cr[..., None] * s5_b_im + ci[..., None] * s5_b_re
    eye = jnp.eye(S5_CLUSTER_GROUPS, dtype=F32)
    cg = S5_CLUSTER_GROUPS

    def pack_b(bb):
        bb = bb.reshape(DEPTH, S5_CLUSTERS, cg, S5_STATE, S5_GROUP)
        return jnp.einsum("ljgnc,gh->ljgchn", bb, eye).reshape(DEPTH, S5_CLUSTERS, cg * S5_GROUP, cg * S5_STATE)

    def pack_c(cc):
        cc = cc.reshape(DEPTH, S5_CLUSTERS, cg, S5_GROUP, S5_STATE)
        return jnp.einsum("ljgcn,gh->ljgnhc", cc, eye).reshape(DEPTH, S5_CLUSTERS, cg * S5_STATE, cg * S5_GROUP)

    p["bc"] = jnp.concatenate([pack_b(bb_re), pack_b(bb_im)], axis=-1).astype(BF16)
    p["cc"] = jnp.concatenate([pack_c(s5_c_re), pack_c(-s5_c_im)], axis=-2).astype(BF16)
    p["lam"] = jnp.stack([lr.reshape(DEPTH, S5_FLAT), li.reshape(DEPTH, S5_FLAT)], axis=1)
    p["dskB"] = s5_d[:, None, :]
    p["glu_w"] = s5_glu_w.astype(BF16)
    p["glu_b"] = s5_glu_b[:, None, :]
    return p


def _group_cfg(B, T):
    if T % 512 == 0:
        return dict(Tp=None, ssd=(1, 256, 256, 2), ml=(1, 256, 256, 2), xa=(1, 256), s5=(8, 32), tm=256)
    assert T == 4 and B % 32 == 0
    return dict(Tp=True, ssd=(8, 16, T, 1), ml=(4, 32, T, 1), xa=(4, 8), s5=(32, T), tm=B * T)


def _pad_time(h, Lp):
    return jnp.pad(h, ((0, 0), (0, Lp - h.shape[1]), (0, 0)))


def _layer(p, l, x, h, h_tm, mk, mv, lkv, states, ls, prev, g_next, last):
    B, T, _ = x.shape
    cfg = _group_cfg(B, T)
    padded = cfg["Tp"] is not None
    conv0, ssd0, sre0, sim0, c0, n0, m0 = states
    prev = prev or dict(a=None, b=None, c=None)

    h2d = h.reshape(B * T, D_MODEL)
    pj = (lambda w, tn: _proj_call(h2d, w, l, tn).reshape(B, T, -1)) if padded else (lambda w, tn: None)

    nb, L, valid, nsub = cfg["ssd"]
    hin = _pad_time(h, L) if padded else h
    ya, st_a = _ssd_call(hin, p, conv0, ssd0, l, ls, prev["a"], nb=nb, L=L, valid=valid, nsub=nsub,
                         proj=pj(p["wA"], WA_COLS // 3))
    nb, L, valid, nsub = cfg["ml"]
    hin = _pad_time(h, L) if padded else h
    yc, st_c = _mlstm_call(hin, p, c0, n0, m0, l, ls, prev["c"], nb=nb, L=L, valid=valid, nsub=nsub)
    nb, L = cfg["xa"]
    if padded:
        yd = _xattn_rows_call(pj(p["wD"], WD_COLS // 2), mk, mv, lkv, nb=nb, L=L)
    else:
        yd = _xattn_call(h, p["wD"], mk, mv, l, lkv, nb=nb, L=L)
    nb, Lt = cfg["s5"]
    NB = B // nb
    yb, st_b = _s5_call(h_tm, p, sre0, sim0, l, ls, prev["b"], nb=nb, Lt=Lt)
    new_states = dict(a=st_a, b=st_b, c=st_c)

    if padded:
        flat = lambda a: a[:, :T].reshape(1, B * T, D_MODEL)
        yb_flat = yb.reshape(NB, T, nb, D_MODEL).transpose(0, 2, 1, 3).reshape(B * T, D_MODEL)
        outs = _merge_call(h.reshape(1, B * T, D_MODEL), flat(ya), yb_flat, flat(yc), flat(yd),
                           x.reshape(1, B * T, D_MODEL), p["wE"], p["b_gate"], p["w_down"], p["w_out"], g_next, l,
                           tm=cfg["tm"], last=last)
        if last:
            res = (outs[0].reshape(B, T, D_MODEL), None, None)
        else:
            hn = outs[1].reshape(B, T, D_MODEL)
            hn_tm = hn.reshape(NB, nb, T, D_MODEL).transpose(0, 2, 1, 3).reshape(NB, T * nb, D_MODEL)
            res = (outs[0].reshape(B, T, D_MODEL), hn, hn_tm)
    else:
        outs = _merge_call(h, ya, yb.reshape(T, B * D_MODEL), yc, yd, x, p["wE"], p["b_gate"], p["w_down"],
                           p["w_out"], g_next, l, tm=cfg["tm"], last=last)
        if last:
            res = (outs[0], None, None)
        else:
            res = (outs[0], outs[1], outs[2].reshape(1, T * B, D_MODEL))
    return res, new_states


def _first_norm(x, g):
    B, T, _ = x.shape
    cfg = _group_cfg(B, T)
    if cfg["Tp"] is None:
        h, h_tm = _norm_call(x, g, cfg["tm"])
        return h, h_tm.reshape(1, T * B, D_MODEL)
    h, _ = _norm_call(x.reshape(1, B * T, D_MODEL), g, B * T)
    h = h.reshape(B, T, D_MODEL)
    nb = cfg["s5"][0]
    h_tm = h.reshape(B // nb, nb, T, D_MODEL).transpose(0, 2, 1, 3).reshape(B // nb, T * nb, D_MODEL)
    return h, h_tm


def kernel(x_prompt, x_sample, mem_prompt, cache_mem_k, cache_mem_v, state_ssd_conv, state_ssd, state_s5_re,
           state_s5_im, state_mlstm_c, state_mlstm_n, state_mlstm_m, norm_in, w_in, b_gate, b_igate, b_fgate,
           ssd_conv_w, ssd_conv_b, ssd_dt_bias, ssd_a_log, ssd_d, ssd_norm, s5_a_re, s5_a_im, s5_log_dt, s5_b_re,
           s5_b_im, s5_c_re, s5_c_im, s5_d, s5_glu_w, s5_glu_b, ml_norm, mem_norm, w_mem_kv, w_down, w_out,
           final_norm):
    p = _pack_params(w_in, b_gate, b_igate, b_fgate, ssd_conv_w, ssd_conv_b, ssd_dt_bias, ssd_a_log, ssd_d,
                     ssd_norm, s5_a_re, s5_a_im, s5_log_dt, s5_b_re, s5_b_im, s5_c_re, s5_c_im, s5_d, s5_glu_w,
                     s5_glu_b, ml_norm, w_down, w_out)
    Bp, Tp, _ = x_prompt.shape
    Bs, Ts, _ = x_sample.shape

    mk_p, mv_p, mk_out, mv_out = _memkv_call(mem_prompt, mem_norm[:, None, :], w_mem_kv.astype(BF16))

    zeros = lambda *s: jnp.zeros((1,) + s, F32)
    states_p = (zeros(Bp, SSD_CONV - 1, SSD_CONV_DIM), zeros(Bp, SSD_HEADS, SSD_HEAD_DIM, SSD_STATE),
                zeros(Bp * S5_GROUPS, S5_STATE), zeros(Bp * S5_GROUPS, S5_STATE),
                zeros(Bp, ML_HEADS, ML_HEAD_DIM, ML_HEAD_DIM), zeros(Bp, ML_HEADS, ML_HEAD_DIM),
                zeros(Bp, 1, LANES))
    m_pad = jnp.pad(state_mlstm_m, ((0, 0), (0, 0), (0, LANES - ML_HEADS)))[:, :, None, :]
    states_s = (state_ssd_conv, state_ssd, state_s5_re.reshape(DEPTH, Bs * S5_GROUPS, S5_STATE),
                state_s5_im.reshape(DEPTH, Bs * S5_GROUPS, S5_STATE), state_mlstm_c, state_mlstm_n, m_pad)

    g0 = norm_in[0][None, :]
    xp, (hp, hp_tm) = x_prompt, _first_norm(x_prompt, g0)
    xs, (hs, hs_tm) = x_sample, _first_norm(x_sample, g0)
    st_p = st_s = None
    for l in range(DEPTH):
        last = l == DEPTH - 1
        g_next = final_norm[None, :] if last else norm_in[l + 1][None, :]
        (xp, hp, hp_tm), st_p = _layer(p, l, xp, hp, hp_tm, mk_p, mv_p, l, states_p, 0, st_p, g_next, last)
        (xs, hs, hs_tm), st_s = _layer(p, l, xs, hs, hs_tm, cache_mem_k, cache_mem_v, l, states_s, l, st_s,
                                       g_next, last)

    def unpack(st, B):
        conv, ssd = st["a"]
        sre, sim = st["b"]
        c, n, m = st["c"]
        return (conv, ssd, sre.reshape(DEPTH, B, S5_GROUPS, S5_STATE), sim.reshape(DEPTH, B, S5_GROUPS, S5_STATE),
                c, n, m[:, :, 0, :ML_HEADS])

    return (xp, xs, mk_out, mv_out) + unpack(st_p, Bp) + unpack(st_s, Bs)
```

```python
import functools
import math

import jax
import jax.numpy as jnp
from jax import lax
from jax.experimental import pallas as pl
from jax.experimental.pallas import tpu as pltpu

F32 = jnp.float32
BF16 = jnp.bfloat16
NEG_INF = float("-inf")

D_MODEL = 1024
DEPTH = 4
BR_WIDTH = D_MODEL
SSD_HEADS = 16
SSD_HEAD_DIM = 64
SSD_GROUPS = 2
SSD_STATE = 128
SSD_CONV = 4
SSD_CONV_DIM = BR_WIDTH + 2 * SSD_GROUPS * SSD_STATE
S5_GROUP = 16
S5_GROUPS = 64
S5_STATE = 64
S5_CLUSTERS = 4
S5_CLUSTER_GROUPS = S5_GROUPS // S5_CLUSTERS
S5_FLAT = S5_GROUPS * S5_STATE
ML_HEADS = 4
ML_HEAD_DIM = 256
MEM_TOKENS = 256
XA_HEADS = 4
XA_HEAD_DIM = 256
EPS = 1e-6
LANES = 128
CONV_PAD = 8

IN_NAMES = ("z_ssd", "xbc", "dt", "u_s5", "z_s5", "q", "k", "v", "i", "f", "o", "z_ml", "q_xa", "z_xa", "gate")
IN_SIZES = (BR_WIDTH, SSD_CONV_DIM, SSD_HEADS, BR_WIDTH, BR_WIDTH, BR_WIDTH, BR_WIDTH, BR_WIDTH,
            ML_HEADS, ML_HEADS, BR_WIDTH, BR_WIDTH, BR_WIDTH, BR_WIDTH, 4 * D_MODEL)

WA_COLS = SSD_CONV_DIM + BR_WIDTH + LANES
WB_COLS = 2 * BR_WIDTH
WC_COLS = 3 * BR_WIDTH + 2 * LANES + 2 * BR_WIDTH
WD_COLS = 2 * BR_WIDTH

VMEM_LIMIT = 56 * 1024 * 1024


def _dot(a, b):
    return jnp.dot(a, b, preferred_element_type=F32)


def _dot_nt(a, b):
    return lax.dot_general(a, b, (((1,), (1,)), ((), ())), preferred_element_type=F32)


def _split3(x):
    hi = x.astype(BF16)
    r = x - hi.astype(F32)
    mid = r.astype(BF16)
    lo = (r - mid.astype(F32)).astype(BF16)
    return hi, mid, lo


def _sel_left(sel, x):
    hi, mid, lo = _split3(x)
    return _dot(sel, hi) + _dot(sel, mid) + _dot(sel, lo)


def _sel_right(x, sel):
    hi, mid, lo = _split3(x)
    return _dot(hi, sel) + _dot(mid, sel) + _dot(lo, sel)


def _softplus(x):
    return jnp.maximum(x, 0.0) + jnp.log1p(jnp.exp(-jnp.abs(x)))


def _silu(x):
    return x * jax.nn.sigmoid(x)


def _rms(x, g):
    return x * lax.rsqrt(jnp.mean(x * x, axis=-1, keepdims=True) + EPS) * g


def _seq_masks(nb, L):
    M = nb * L
    ri = lax.broadcasted_iota(jnp.int32, (M, M), 0)
    ci = lax.broadcasted_iota(jnp.int32, (M, M), 1)
    if nb > 1:
        sh = int(math.log2(L))
        same = lax.shift_right_logical(ri, sh) == lax.shift_right_logical(ci, sh)
        causal = same & (ci <= ri)
        upper = same & (ri <= ci)
    else:
        same = None
        causal = ci <= ri
        upper = ri <= ci
    return same, causal, upper


def _valid_masks(nb, L, valid):
    M = nb * L
    rv = (lax.broadcasted_iota(jnp.int32, (M, 1), 0) & (L - 1)) < valid
    cv = (lax.broadcasted_iota(jnp.int32, (1, M), 1) & (L - 1)) < valid
    return rv, cv


def _row_sel(nb, L, b, dtype):
    rows = lax.broadcasted_iota(jnp.int32, (nb * L, 1), 0)
    return (lax.shift_right_logical(rows, int(math.log2(L))) == b).astype(dtype)


def _col_pieces(n_cols, width):
    return [(c0, min(c0 + width, n_cols)) for c0 in range(0, n_cols, width)]


class _Spread:
    def __init__(self, thunks, n_slots):
        self.thunks, self.n_slots, self.calls, self.done = list(thunks), n_slots, 0, 0

    def slot(self):
        self.calls += 1
        while self.done < len(self.thunks) and self.done * self.n_slots < self.calls * len(self.thunks):
            self.thunks[self.done]()
            self.done += 1

    def flush(self):
        while self.done < len(self.thunks):
            self.thunks[self.done]()
            self.done += 1


def _proj_piece(proj_scr, slot, hb, w_ref, c0, c1):
    proj_scr[slot, :, c0:c1] = _dot(hb, w_ref[:, c0:c1])


def _next_chunk_spec(nb, L, nsub, n_steps, n_blocks):
    def index(i, c):
        nxt = jnp.minimum(i * n_steps + c + 1, n_blocks * n_steps - 1)
        return (nxt // n_steps, (nxt % n_steps) * nsub, 0)
    return pl.BlockSpec((nb, L, D_MODEL), index)


def _fill_padded_proj(proj_scr, p_ref, nb, L, valid):
    @pl.when((pl.program_id(0) == 0) & (pl.program_id(1) == 0))
    def _zero():
        proj_scr[0] = jnp.zeros(proj_scr.shape[1:], F32)

    for b in range(nb):
        proj_scr[0, b * L:b * L + valid, :] = p_ref[b]


def _proj_body(h_ref, w_ref, o_ref):
    o_ref[...] = _dot(h_ref[...], w_ref[...])


def _proj_call(h2d, w, l, tn):
    R = h2d.shape[0]
    cols = w.shape[-1]
    return pl.pallas_call(
        _proj_body,
        grid=(cols // tn,),
        in_specs=[pl.BlockSpec((R, D_MODEL), lambda j: (0, 0)),
                  pl.BlockSpec((None, D_MODEL, tn), lambda j: (l, 0, j))],
        out_specs=pl.BlockSpec((R, tn), lambda j: (0, j)),
        out_shape=jax.ShapeDtypeStruct((R, cols), F32),
        compiler_params=pltpu.CompilerParams(dimension_semantics=("arbitrary",), vmem_limit_bytes=VMEM_LIMIT),
        name="in_proj",
    )(h2d, w)


def _layer_spec(shape, l):
    return pl.BlockSpec((None,) + shape, lambda *_: (l,) + (0,) * len(shape), pipeline_mode=pl.Buffered(1))


def _state_spec(shape, l):
    return pl.BlockSpec((None,) + shape, lambda i, c: (l, i) + (0,) * (len(shape) - 1))


def _branch_call(body, *, name, grid, in_specs, args, y_spec, y_shape, st_specs, st_shapes, prev, scratch):
    n_in = len(args)
    prev = () if prev is None else tuple(prev)

    def wrapped(*refs):
        body(*refs[:n_in], *refs[n_in + len(prev):])

    outs = pl.pallas_call(
        wrapped,
        grid=grid,
        in_specs=list(in_specs) + [pl.BlockSpec(memory_space=pl.ANY)] * len(prev),
        out_specs=[y_spec] + list(st_specs),
        out_shape=[y_shape] + list(st_shapes),
        scratch_shapes=scratch,
        input_output_aliases={n_in + k: 1 + k for k in range(len(prev))},
        compiler_params=pltpu.CompilerParams(dimension_semantics=("arbitrary", "arbitrary"),
                                             vmem_limit_bytes=VMEM_LIMIT),
        name=name,
    )(*args, *prev)
    return outs[0], tuple(outs[1:])


def _norm_body(x_ref, g_ref, h_ref):
    h_ref[0] = _rms(x_ref[0], g_ref[...]).astype(BF16)


def _norm_call(x, g, tm):
    B, T, _ = x.shape
    return pl.pallas_call(
        _norm_body,
        grid=(B, T // tm),
        in_specs=[pl.BlockSpec((1, tm, D_MODEL), lambda b, t: (b, t, 0)),
                  pl.BlockSpec((1, D_MODEL), lambda b, t: (0, 0))],
        out_specs=pl.BlockSpec((1, tm, D_MODEL), lambda b, t: (b, t, 0)),
        out_shape=jax.ShapeDtypeStruct((B, T, D_MODEL), BF16),
        compiler_params=pltpu.CompilerParams(dimension_semantics=("arbitrary", "arbitrary")),
        name="rmsnorm_in",
    )(x, g)


def _memkv_body(mem_ref, g_ref, w_ref, mk_ref, mv_ref, mk5_ref, mv5_ref):
    hb = _rms(mem_ref[0], g_ref[0]).astype(BF16)
    kv = _dot(hb, w_ref[0])
    mk_ref[0, 0] = kv[:, :BR_WIDTH]
    mv_ref[0, 0] = kv[:, BR_WIDTH:]
    for hd in range(XA_HEADS):
        lo, hi = hd * XA_HEAD_DIM, (hd + 1) * XA_HEAD_DIM
        mk5_ref[0, 0, :, hd, :] = kv[:, lo:hi]
        mv5_ref[0, 0, :, hd, :] = kv[:, BR_WIDTH + lo:BR_WIDTH + hi]


def _memkv_call(mem, g, w_kv):
    B = mem.shape[0]
    out = jax.ShapeDtypeStruct((DEPTH, B, MEM_TOKENS, BR_WIDTH), F32)
    out5 = jax.ShapeDtypeStruct((DEPTH, B, MEM_TOKENS, XA_HEADS, XA_HEAD_DIM), F32)
    spec5 = pl.BlockSpec((1, 1, MEM_TOKENS, XA_HEADS, XA_HEAD_DIM), lambda l, b: (l, b, 0, 0, 0))
    return pl.pallas_call(
        _memkv_body,
        grid=(DEPTH, B),
        in_specs=[pl.BlockSpec((1, MEM_TOKENS, D_MODEL), lambda l, b: (b, 0, 0)),
                  pl.BlockSpec((1, 1, D_MODEL), lambda l, b: (l, 0, 0)),
                  pl.BlockSpec((1, D_MODEL, 2 * BR_WIDTH), lambda l, b: (l, 0, 0))],
        out_specs=[pl.BlockSpec((1, 1, MEM_TOKENS, BR_WIDTH), lambda l, b: (l, b, 0, 0)),
                   pl.BlockSpec((1, 1, MEM_TOKENS, BR_WIDTH), lambda l, b: (l, b, 0, 0)), spec5, spec5],
        out_shape=[out, out, out5, out5],
        compiler_params=pltpu.CompilerParams(dimension_semantics=("arbitrary", "arbitrary"),
                                             vmem_limit_bytes=VMEM_LIMIT),
        name="mem_kv",
    )(mem, g, w_kv)


def _ssd_body(h_ref, hn_ref, w_ref, wst_ref, prow_ref, pcol_ref, cw_ref, cb_ref, dsk_ref, nrm_ref, buf0_ref,
              s0_ref, y_ref, bufo_ref, so_ref, xp_scr, yin_scr, yst_scr, proj_scr, *, nb, L, valid, nsub):
    M = nb * L
    pipelined = nsub % 2 == 0
    chunk = lambda ref, s: ref[:, s * L:(s + 1) * L, :].reshape(M, D_MODEL)

    @pl.when(pl.program_id(1) == 0)
    def _init():
        so_ref[...] = s0_ref[...]
        xp_scr[:, CONV_PAD - 3:CONV_PAD, :] = buf0_ref[...]

    if pipelined:
        @pl.when((pl.program_id(0) == 0) & (pl.program_id(1) == 0))
        def _prologue():
            proj_scr[0] = _dot(chunk(h_ref, 0), w_ref[...])
    else:
        assert nsub == 1
        _fill_padded_proj(proj_scr, w_ref, nb, L, valid)

    for s in range(nsub):
        hb_next = chunk(h_ref, s + 1) if s + 1 < nsub else (hn_ref[...].reshape(M, D_MODEL) if pipelined else None)
        thunks = [] if hb_next is None else [
            functools.partial(_proj_piece, proj_scr, (s + 1) % 2, hb_next, w_ref, c0, c1)
            for c0, c1 in _col_pieces(WA_COLS, 512)]
        _ssd_chunk(chunk(h_ref, s), proj_scr.at[s % 2], s, _Spread(thunks, SSD_HEADS * (1 + nb)),
                   wst_ref, prow_ref, pcol_ref, cw_ref, cb_ref, dsk_ref, nrm_ref,
                   y_ref, bufo_ref, so_ref, xp_scr, yin_scr, yst_scr, nb=nb, L=L, valid=valid)


def _ssd_chunk(hb, proj, s, spread, wst_ref, prow_ref, pcol_ref, cw_ref, cb_ref, dsk_ref, nrm_ref,
               y_ref, bufo_ref, so_ref, xp_scr, yin_scr, yst_scr, *, nb, L, valid):
    M = nb * L
    hg = SSD_HEADS // SSD_GROUPS

    xp_scr[:, CONV_PAD:CONV_PAD + L, :] = proj[:, :SSD_CONV_DIM].reshape(nb, L, SSD_CONV_DIM)
    conv = cb_ref[...][None]
    for k in range(SSD_CONV):
        o = CONV_PAD - 3 + k
        conv = conv + xp_scr[:, o:o + L, :] * cw_ref[k:k + 1, :][None]
    nbuf = xp_scr[:, CONV_PAD - 3 + valid:CONV_PAD + valid, :]
    xp_scr[:, CONV_PAD - 3:CONV_PAD, :] = nbuf
    bufo_ref[...] = nbuf
    xbc = _silu(conv).reshape(M, SSD_CONV_DIM)
    xs = xbc[:, :BR_WIDTH]
    bm = xbc[:, BR_WIDTH:BR_WIDTH + SSD_GROUPS * SSD_STATE].astype(BF16)
    cm = xbc[:, BR_WIDTH + SSD_GROUPS * SSD_STATE:].astype(BF16)
    z = proj[:, SSD_CONV_DIM:SSD_CONV_DIM + BR_WIDTH]

    same, causal, upper = _seq_masks(nb, L)
    tril_b = causal.astype(BF16)
    triu_b = upper.astype(BF16)
    same_b = jnp.ones((M, M), BF16) if same is None else same.astype(BF16)

    dt = _softplus(proj[:, SSD_CONV_DIM + BR_WIDTH:] + prow_ref[0:1, :])
    dtT = _softplus(_dot_nt(wst_ref[...], hb) + pcol_ref[:, 0:1])
    dt = jnp.where(lax.broadcasted_iota(jnp.int32, (1, LANES), 1) < SSD_HEADS, dt, 0.0)
    if valid < L:
        rv, cv = _valid_masks(nb, L, valid)
        dt = jnp.where(rv, dt, 0.0)
        dtT = jnp.where(cv, dtT, 0.0)
    la = dt * (-jnp.exp(prow_ref[1:2, :]))
    laT = dtT * (-jnp.exp(pcol_ref[:, 1:2]))
    acs = _sel_left(tril_b, la)
    tot = _sel_left(same_b, la)
    acsT = _sel_right(laT, triu_b)

    totT = _sel_right(laT, same_b)
    tailT = jnp.exp(totT - acsT)
    eend = jnp.exp(tot)
    expand = (lax.shift_right_logical(lax.broadcasted_iota(jnp.int32, (LANES, BR_WIDTH), 1), 6)
              == lax.broadcasted_iota(jnp.int32, (LANES, BR_WIDTH), 0)).astype(BF16)
    eacs_x = _sel_right(jnp.exp(acs), expand)

    xsT = xs.T
    wT = dtT * tailT

    for g in range(SSD_GROUPS):
        cm_g = cm[:, g * SSD_STATE:(g + 1) * SSD_STATE]
        bm_g = bm[:, g * SSD_STATE:(g + 1) * SSD_STATE]
        cb = _dot_nt(cm_g, bm_g)
        ats = []
        for hh in range(hg):
            h = g * hg + hh
            lo, hi = h * SSD_HEAD_DIM, (h + 1) * SSD_HEAD_DIM
            seg = acs[:, h:h + 1] - acsT[h:h + 1, :]
            decay = jnp.exp(jnp.where(causal, seg, NEG_INF))
            xdtT = (xsT[lo:hi, :] * dtT[h:h + 1, :]).astype(BF16)
            ats.append((xsT[lo:hi, :] * wT[h:h + 1, :]).astype(BF16))
            yin_scr[:, lo:hi] = _dot_nt((cb * decay).astype(BF16), xdtT)
            spread.slot()
        at_g = jnp.concatenate(ats, axis=0)
        glo, ghi = g * hg, (g + 1) * hg
        for b in range(nb):
            r0, r1 = b * L, (b + 1) * L
            bm_b = bm_g if nb == 1 else bm_g * _row_sel(nb, L, b, BF16)
            s_old = so_ref[b, glo:ghi].reshape(hg * SSD_HEAD_DIM, SSD_STATE)
            yst_scr[r0:r1, glo * SSD_HEAD_DIM:ghi * SSD_HEAD_DIM] = _dot_nt(cm_g[r0:r1, :], s_old.astype(BF16))
            kept = jnp.concatenate(
                [eend[r0:r0 + 1, glo + hh:glo + hh + 1] * s_old[hh * SSD_HEAD_DIM:(hh + 1) * SSD_HEAD_DIM]
                 for hh in range(hg)], axis=0)
            so_ref[b, glo:ghi] = (kept + _dot(at_g, bm_b)).reshape(hg, SSD_HEAD_DIM, SSD_STATE)
            for _ in range(hg):
                spread.slot()
    spread.flush()

    y = yin_scr[...] + eacs_x * yst_scr[...] + dsk_ref[...] * xs
    y_ref[:, s * L:(s + 1) * L, :] = _rms(y * _silu(z), nrm_ref[...]).reshape(nb, L, BR_WIDTH).astype(BF16)


def _ssd_call(h, p, buf0, s0, l, ls, prev, *, nb, L, valid, nsub, proj=None):
    B, T, _ = h.shape
    wspec = lambda shape: _layer_spec(shape, l)
    conv_blk = (nb, SSD_CONV - 1, SSD_CONV_DIM)
    ssd_blk = (nb, SSD_HEADS, SSD_HEAD_DIM, SSD_STATE)
    w_arg, w_spec = ((p["wA"], wspec((D_MODEL, WA_COLS))) if proj is None else
                     (proj, pl.BlockSpec((nb, valid, WA_COLS), lambda i, c: (i, 0, 0))))
    return _branch_call(
        functools.partial(_ssd_body, nb=nb, L=L, valid=valid, nsub=nsub),
        name="ssd_branch",
        grid=(B // nb, T // (nsub * L)),
        in_specs=[pl.BlockSpec((nb, nsub * L, D_MODEL), lambda i, c: (i, c, 0)),
                  _next_chunk_spec(nb, L, nsub, T // (nsub * L), B // nb),
                  w_spec, wspec((SSD_HEADS, D_MODEL)),
                  wspec((8, LANES)), wspec((SSD_HEADS, 2)),
                  wspec((SSD_CONV, SSD_CONV_DIM)), wspec((1, SSD_CONV_DIM)),
                  wspec((1, BR_WIDTH)), wspec((1, BR_WIDTH)),
                  _state_spec(conv_blk, ls), _state_spec(ssd_blk, ls)],
        args=(h, h, w_arg, p["wstA"], p["prowA"], p["pcolA"], p["conv_w"], p["conv_b"], p["dskA"], p["nrmA"],
              buf0, s0),
        y_spec=pl.BlockSpec((nb, nsub * L, BR_WIDTH), lambda i, c: (i, c, 0)),
        y_shape=jax.ShapeDtypeStruct((B, T, BR_WIDTH), BF16),
        st_specs=[_state_spec(conv_blk, l), _state_spec(ssd_blk, l)],
        st_shapes=[jax.ShapeDtypeStruct((DEPTH, B) + conv_blk[1:], F32),
                   jax.ShapeDtypeStruct((DEPTH, B) + ssd_blk[1:], F32)],
        prev=prev,
        scratch=[pltpu.VMEM((nb, L + CONV_PAD, SSD_CONV_DIM), F32),
                 pltpu.VMEM((nb * L, BR_WIDTH), F32),
                 pltpu.VMEM((nb * L, BR_WIDTH), F32),
                 pltpu.VMEM((2, nb * L, WA_COLS), F32)])


S5_SCAN_LANES = 512


def _s5_body(h_ref, w_ref, bc_ref, cc_ref, lam_ref, dsk_ref, gw_ref, gb_ref, sre0_ref, sim0_ref,
             y_ref, sre_ref, sim_ref, hs_scr, st_scr, *, nb, Lt):
    half = S5_CLUSTER_GROUPS * S5_STATE

    @pl.when(pl.program_id(1) == 0)
    def _load_state():
        for g in range(S5_GROUPS):
            st_scr[0, :, g * S5_STATE:(g + 1) * S5_STATE] = sre0_ref[pl.ds(g, nb, stride=S5_GROUPS), :]
            st_scr[1, :, g * S5_STATE:(g + 1) * S5_STATE] = sim0_ref[pl.ds(g, nb, stride=S5_GROUPS), :]

    hb = jnp.swapaxes(h_ref[...].astype(F32), 0, 1).reshape(nb * Lt, D_MODEL).astype(BF16)
    proj = _dot(hb, w_ref[...])
    u = proj[:, :BR_WIDTH]
    z = proj[:, BR_WIDTH:]
    ub = u.astype(BF16)
    gl = S5_CLUSTER_GROUPS * S5_GROUP
    ys = []
    for j in range(S5_CLUSTERS):
        hs_scr[:, 2 * half * j:2 * half * (j + 1)] = _dot(ub[:, gl * j:gl * (j + 1)], bc_ref[j])
        for q in range(half // S5_SCAN_LANES):
            cr = 2 * half * j + S5_SCAN_LANES * q
            ci = cr + half
            sc = half * j + S5_SCAN_LANES * q
            lr = jnp.broadcast_to(lam_ref[0:1, sc:sc + S5_SCAN_LANES], (8, S5_SCAN_LANES))
            li = jnp.broadcast_to(lam_ref[1:2, sc:sc + S5_SCAN_LANES], (8, S5_SCAN_LANES))
            for sg in range(nb // 8):
                hr = st_scr[0, 8 * sg:8 * sg + 8, sc:sc + S5_SCAN_LANES]
                hi = st_scr[1, 8 * sg:8 * sg + 8, sc:sc + S5_SCAN_LANES]
                for t in range(Lt):
                    r0 = t * nb + 8 * sg
                    nr = lr * hr - li * hi + hs_scr[r0:r0 + 8, cr:cr + S5_SCAN_LANES]
                    ni = lr * hi + li * hr + hs_scr[r0:r0 + 8, ci:ci + S5_SCAN_LANES]
                    hs_scr[r0:r0 + 8, cr:cr + S5_SCAN_LANES] = nr
                    hs_scr[r0:r0 + 8, ci:ci + S5_SCAN_LANES] = ni
                    hr, hi = nr, ni
                st_scr[0, 8 * sg:8 * sg + 8, sc:sc + S5_SCAN_LANES] = hr
                st_scr[1, 8 * sg:8 * sg + 8, sc:sc + S5_SCAN_LANES] = hi
        ys.append(_dot(hs_scr[:, 2 * half * j:2 * half * (j + 1)].astype(BF16), cc_ref[j]))

    y = jnp.concatenate(ys, axis=1) + dsk_ref[...] * u
    yb = jax.nn.gelu(y)
    glu = jax.nn.sigmoid(_dot(yb.astype(BF16), gw_ref[...]) + gb_ref[...])
    out = (yb * glu * _silu(z)).reshape(Lt, nb, BR_WIDTH)
    y_ref[...] = jnp.swapaxes(out, 0, 1).astype(BF16)

    @pl.when(pl.program_id(1) == pl.num_programs(1) - 1)
    def _store_state():
        for g in range(S5_GROUPS):
            sre_ref[pl.ds(g, nb, stride=S5_GROUPS), :] = st_scr[0, :, g * S5_STATE:(g + 1) * S5_STATE]
            sim_ref[pl.ds(g, nb, stride=S5_GROUPS), :] = st_scr[1, :, g * S5_STATE:(g + 1) * S5_STATE]


def _s5_call(h, p, sre0, sim0, l, ls, prev, *, nb, Lt):
    B, T, _ = h.shape
    wspec = lambda shape: _layer_spec(shape, l)
    st_blk = (nb * S5_GROUPS, S5_STATE)
    st_shape = jax.ShapeDtypeStruct((DEPTH, B * S5_GROUPS, S5_STATE), F32)
    return _branch_call(
        functools.partial(_s5_body, nb=nb, Lt=Lt),
        name="s5_branch",
        grid=(B // nb, T // Lt),
        in_specs=[pl.BlockSpec((nb, Lt, D_MODEL), lambda i, c: (i, c, 0)),
                  wspec((D_MODEL, WB_COLS)),
                  wspec((S5_CLUSTERS, S5_CLUSTER_GROUPS * S5_GROUP, 2 * S5_CLUSTER_GROUPS * S5_STATE)),
                  wspec((S5_CLUSTERS, 2 * S5_CLUSTER_GROUPS * S5_STATE, S5_CLUSTER_GROUPS * S5_GROUP)),
                  wspec((2, S5_FLAT)), wspec((1, BR_WIDTH)),
                  wspec((BR_WIDTH, BR_WIDTH)), wspec((1, BR_WIDTH)),
                  _state_spec(st_blk, ls), _state_spec(st_blk, ls)],
        args=(h, p["wB"], p["bc"], p["cc"], p["lam"], p["dskB"], p["glu_w"], p["glu_b"], sre0, sim0),
        y_spec=pl.BlockSpec((nb, Lt, BR_WIDTH), lambda i, c: (i, c, 0)),
        y_shape=jax.ShapeDtypeStruct((B, T, BR_WIDTH), BF16),
        st_specs=[_state_spec(st_blk, l), _state_spec(st_blk, l)],
        st_shapes=[st_shape, st_shape],
        prev=prev,
        scratch=[pltpu.VMEM((nb * Lt, 2 * S5_FLAT), F32), pltpu.VMEM((2, nb, S5_FLAT), F32)])


def _mlstm_body(h_ref, hn_ref, w_ref, wst_ref, prow_ref, pcol_ref, nrm_ref, c0_ref, n0_ref, m0_ref,
                y_ref, co_ref, no_ref, mo_ref, num_scr, proj_scr, *, nb, L, valid, nsub):
    M = nb * L
    pipelined = nsub % 2 == 0
    chunk = lambda ref, s: ref[:, s * L:(s + 1) * L, :].reshape(M, D_MODEL)

    @pl.when(pl.program_id(1) == 0)
    def _init():
        co_ref[...] = c0_ref[...]
        no_ref[...] = n0_ref[...]
        mo_ref[...] = m0_ref[...]

    if pipelined:
        @pl.when((pl.program_id(0) == 0) & (pl.program_id(1) == 0))
        def _prologue():
            proj_scr[0] = _dot(chunk(h_ref, 0), w_ref[...])
    else:
        assert nsub == 1
        proj_scr[0] = _dot(chunk(h_ref, 0), w_ref[...])

    for s in range(nsub):
        hb_next = chunk(h_ref, s + 1) if s + 1 < nsub else (hn_ref[...].reshape(M, D_MODEL) if pipelined else None)
        thunks = [] if hb_next is None else [
            functools.partial(_proj_piece, proj_scr, (s + 1) % 2, hb_next, w_ref, c0, c1)
            for c0, c1 in _col_pieces(WC_COLS, 768)]
        _mlstm_chunk(chunk(h_ref, s), proj_scr.at[s % 2], s, _Spread(thunks, ML_HEADS * (1 + nb)),
                     wst_ref, prow_ref, pcol_ref, nrm_ref, y_ref, co_ref, no_ref, mo_ref, num_scr,
                     nb=nb, L=L, valid=valid)


def _mlstm_chunk(hb, proj, s, spread, wst_ref, prow_ref, pcol_ref, nrm_ref, y_ref, co_ref, no_ref, mo_ref,
                 num_scr, *, nb, L, valid):
    M = nb * L
    W = BR_WIDTH
    q = proj[:, :W]
    k = proj[:, W:2 * W] * (ML_HEAD_DIM ** -0.5)
    v = proj[:, 2 * W:3 * W]
    ig = proj[:, 3 * W:3 * W + LANES] + prow_ref[0:1, :]
    logf = -_softplus(-(proj[:, 3 * W + LANES:3 * W + 2 * LANES] + prow_ref[1:2, :]))
    o = proj[:, 3 * W + 2 * LANES:4 * W + 2 * LANES]
    z = proj[:, 4 * W + 2 * LANES:]
    smallT = _dot_nt(wst_ref[...], hb) + pcol_ref[...]
    igT = smallT[0:8, :]
    logfT = -_softplus(-smallT[8:16, :])
    if valid < L:
        rv, cv = _valid_masks(nb, L, valid)
        ig = jnp.where(rv, ig, NEG_INF)
        logf = jnp.where(rv, logf, 0.0)
        igT = jnp.where(cv, igT, NEG_INF)
        logfT = jnp.where(cv, logfT, 0.0)

    same, causal, upper = _seq_masks(nb, L)
    tril_b = causal.astype(BF16)
    triu_b = upper.astype(BF16)
    same_b = jnp.ones((M, M), BF16) if same is None else same.astype(BF16)
    bcum = _sel_left(tril_b, logf)
    tot = _sel_left(same_b, logf)
    bcumT = _sel_right(logfT, triu_b)
    totT = _sel_right(logfT, same_b)
    m_rows = jnp.broadcast_to(mo_ref[...], (nb, L, LANES)).reshape(M, LANES)
    gcar = bcum + m_rows
    g_end = tot + m_rows
    d_end = tot - bcum + ig
    d_endT = totT - bcumT + igT

    qb = q.astype(BF16)
    kb = k.astype(BF16)
    vb = v.astype(BF16)
    lane = lax.broadcasted_iota(jnp.int32, (M, LANES), 1)
    m_new_all = jnp.zeros((M, LANES), F32)
    per_head = []
    wk_parts = []
    for hd in range(ML_HEADS):
        lo, hi = hd * ML_HEAD_DIM, (hd + 1) * ML_HEAD_DIM
        dmat = jnp.where(causal, bcum[:, hd:hd + 1] - bcumT[hd:hd + 1, :] + igT[hd:hd + 1, :], NEG_INF)
        m_l = jnp.maximum(gcar[:, hd:hd + 1], jnp.max(dmat, axis=1, keepdims=True))
        w_inter = jnp.exp(gcar[:, hd:hd + 1] - m_l)
        qk = _dot_nt(qb[:, lo:hi], kb[:, lo:hi]) * jnp.exp(dmat - m_l)
        num_scr[:, lo:hi] = _dot(qk.astype(BF16), vb[:, lo:hi])
        den_intra = jnp.sum(qk, axis=1, keepdims=True)
        d_row = d_endT[hd:hd + 1, :] if same is None else jnp.where(same, d_endT[hd:hd + 1, :], NEG_INF)
        m_new = jnp.maximum(g_end[:, hd:hd + 1], jnp.max(d_row, axis=1, keepdims=True))
        w_end = jnp.exp(d_end[:, hd:hd + 1] - m_new)
        dec = jnp.exp(g_end[:, hd:hd + 1] - m_new)
        wk_parts.append(w_end * k[:, lo:hi])
        m_new_all = jnp.where(lane == hd, m_new, m_new_all)
        per_head.append((m_l, w_inter, den_intra, dec))
        spread.slot()
    wk = jnp.concatenate(wk_parts, axis=1)
    kt = wk.T.astype(BF16)

    for b in range(nb):
        r0, r1 = b * L, (b + 1) * L
        sel = None if nb == 1 else _row_sel(nb, L, b, BF16)
        for hd in range(ML_HEADS):
            lo, hi = hd * ML_HEAD_DIM, (hd + 1) * ML_HEAD_DIM
            m_l, w_inter, den_intra, dec = per_head[hd]
            c_old = co_ref[b, hd]
            n_old = no_ref[b, hd:hd + 1, :]
            q_r = q[r0:r1, lo:hi]
            wi = w_inter[r0:r1]
            num = num_scr[r0:r1, lo:hi] + wi * _dot(qb[r0:r1, lo:hi], c_old.astype(BF16))
            den = den_intra[r0:r1] + wi * jnp.sum(q_r * n_old, axis=1, keepdims=True)
            hout = num / jnp.maximum(jnp.abs(den), jnp.exp(-m_l[r0:r1]))
            ho = jax.nn.sigmoid(o[r0:r1, lo:hi]) * hout
            yv = _rms(ho, nrm_ref[:, lo:hi]) * _silu(z[r0:r1, lo:hi])
            y_ref[b, s * L:(s + 1) * L, lo:hi] = yv.astype(BF16)
            dec_b = dec[r0:r0 + 1]
            vb_b = vb[:, lo:hi] if sel is None else vb[:, lo:hi] * sel
            co_ref[b, hd] = dec_b * c_old + _dot(kt[lo:hi, :], vb_b)
            no_ref[b, hd:hd + 1, :] = dec_b * n_old + jnp.sum(wk[r0:r1, lo:hi], axis=0, keepdims=True)
            spread.slot()
        mo_ref[b] = m_new_all[r0:r0 + 1, :]
    spread.flush()


def _mlstm_call(h, p, c0, n0, m0, l, ls, prev, *, nb, L, valid, nsub):
    B, T, _ = h.shape
    wspec = lambda shape: _layer_spec(shape, l)
    w_arg, w_spec = p["wC"], wspec((D_MODEL, WC_COLS))
    c_blk = (nb, ML_HEADS, ML_HEAD_DIM, ML_HEAD_DIM)
    n_blk = (nb, ML_HEADS, ML_HEAD_DIM)
    m_blk = (nb, 1, LANES)
    return _branch_call(
        functools.partial(_mlstm_body, nb=nb, L=L, valid=valid, nsub=nsub),
        name="mlstm_branch",
        grid=(B // nb, T // (nsub * L)),
        in_specs=[pl.BlockSpec((nb, nsub * L, D_MODEL), lambda i, c: (i, c, 0)),
                  _next_chunk_spec(nb, L, nsub, T // (nsub * L), B // nb),
                  w_spec, wspec((16, D_MODEL)),
                  wspec((8, LANES)), wspec((16, 1)), wspec((1, BR_WIDTH)),
                  _state_spec(c_blk, ls), _state_spec(n_blk, ls), _state_spec(m_blk, ls)],
        args=(h, h, w_arg, p["wstC"], p["prowC"], p["pcolC"], p["nrmC"], c0, n0, m0),
        y_spec=pl.BlockSpec((nb, nsub * L, BR_WIDTH), lambda i, c: (i, c, 0)),
        y_shape=jax.ShapeDtypeStruct((B, T, BR_WIDTH), BF16),
        st_specs=[_state_spec(c_blk, l), _state_spec(n_blk, l), _state_spec(m_blk, l)],
        st_shapes=[jax.ShapeDtypeStruct((DEPTH, B) + c_blk[1:], F32),
                   jax.ShapeDtypeStruct((DEPTH, B) + n_blk[1:], F32),
                   jax.ShapeDtypeStruct((DEPTH, B) + m_blk[1:], F32)],
        prev=prev,
        scratch=[pltpu.VMEM((nb * L, BR_WIDTH), F32), pltpu.VMEM((2, nb * L, WC_COLS), F32)])


def _xattn_body(h_ref, w_ref, mk_ref, mv_ref, y_ref, *, nb, L):
    M = nb * L
    hb = h_ref[...].reshape(M, D_MODEL)
    proj = _dot(hb, w_ref[...])
    qb = proj[:, :BR_WIDTH].astype(BF16)
    z = proj[:, BR_WIDTH:]
    for b in range(nb):
        r0, r1 = b * L, (b + 1) * L
        for hd in range(XA_HEADS):
            lo, hi = hd * XA_HEAD_DIM, (hd + 1) * XA_HEAD_DIM
            s = _dot_nt(qb[r0:r1, lo:hi], mk_ref[b, :, lo:hi].astype(BF16)) * (XA_HEAD_DIM ** -0.5)
            e = jnp.exp(s - jnp.max(s, axis=1, keepdims=True))
            p = e / jnp.sum(e, axis=1, keepdims=True)
            a = _dot(p.astype(BF16), mv_ref[b, :, lo:hi].astype(BF16))
            y_ref[b, :, lo:hi] = (a * _silu(z[r0:r1, lo:hi])).astype(BF16)


def _xattn_rows_body(p_ref, mk_ref, mv_ref, y_ref, pj_scr, *, nb, L, valid):
    R = MEM_TOKENS * XA_HEADS
    Q = XA_HEADS * L

    @pl.when(pl.program_id(0) == 0)
    def _zero():
        pj_scr[...] = jnp.zeros(pj_scr.shape, F32)

    for b in range(nb):
        pj_scr[b * L:b * L + valid, :] = p_ref[b]
    q = pj_scr[:, :BR_WIDTH]
    z = pj_scr[:, BR_WIDTH:]
    row_head = lax.shift_right_logical(lax.broadcasted_iota(jnp.int32, (Q, R), 0), int(math.log2(L)))
    col_head = lax.broadcasted_iota(jnp.int32, (Q, R), 1) & (XA_HEADS - 1)
    own = row_head == col_head
    for b in range(nb):
        r0, r1 = b * L, (b + 1) * L
        qall = jnp.concatenate([q[r0:r1, hd * XA_HEAD_DIM:(hd + 1) * XA_HEAD_DIM] for hd in range(XA_HEADS)],
                               axis=0).astype(BF16)
        mk2 = mk_ref[b].reshape(R, XA_HEAD_DIM).astype(BF16)
        mv2 = mv_ref[b].reshape(R, XA_HEAD_DIM).astype(BF16)
        s = jnp.where(own, _dot_nt(qall, mk2) * (XA_HEAD_DIM ** -0.5), -1e30)
        e = jnp.exp(s - jnp.max(s, axis=1, keepdims=True))
        pr = e / jnp.sum(e, axis=1, keepdims=True)
        a = _dot(pr.astype(BF16), mv2)
        for hd in range(XA_HEADS):
            lo, hi = hd * XA_HEAD_DIM, (hd + 1) * XA_HEAD_DIM
            yv = a[hd * L:(hd + 1) * L, :] * _silu(z[r0:r1, lo:hi])
            y_ref[b, :, lo:hi] = yv[:valid].astype(BF16)


def _xattn_rows_call(proj, mk, mv, lkv, *, nb, L):
    B, T, _ = proj.shape
    assert T <= L
    kv_spec = pl.BlockSpec((None, nb, MEM_TOKENS, XA_HEADS, XA_HEAD_DIM), lambda i: (lkv, i, 0, 0, 0))
    return pl.pallas_call(
        functools.partial(_xattn_rows_body, nb=nb, L=L, valid=T),
        grid=(B // nb,),
        in_specs=[pl.BlockSpec((nb, T, WD_COLS), lambda i: (i, 0, 0)), kv_spec, kv_spec],
        out_specs=pl.BlockSpec((nb, T, BR_WIDTH), lambda i: (i, 0, 0)),
        out_shape=jax.ShapeDtypeStruct((B, T, BR_WIDTH), BF16),
        scratch_shapes=[pltpu.VMEM((nb * L, WD_COLS), F32)],
        compiler_params=pltpu.CompilerParams(dimension_semantics=("arbitrary",), vmem_limit_bytes=VMEM_LIMIT),
        name="xattn_rows",
    )(proj, mk, mv)


def _xattn_call(h, wD, mk, mv, l, lkv, *, nb, L):
    B, T, _ = h.shape
    grid = (B // nb, T // L)
    kv_spec = pl.BlockSpec((None, nb, MEM_TOKENS, BR_WIDTH), lambda i, c: (lkv, i, 0, 0))
    body = functools.partial(_xattn_body, nb=nb, L=L)
    return pl.pallas_call(
        body,
        grid=grid,
        in_specs=[pl.BlockSpec((nb, L, D_MODEL), lambda i, c: (i, c, 0)),
                  _layer_spec((D_MODEL, WD_COLS), l),
                  kv_spec, kv_spec],
        out_specs=pl.BlockSpec((nb, L, BR_WIDTH), lambda i, c: (i, c, 0)),
        out_shape=jax.ShapeDtypeStruct((B, T, BR_WIDTH), BF16),
        compiler_params=pltpu.CompilerParams(dimension_semantics=("arbitrary", "arbitrary"),
                                             vmem_limit_bytes=VMEM_LIMIT),
        name="xattn_branch",
    )(h, wD, mk, mv)


def _merge_body(h_ref, ya_ref, yb_ref, yc_ref, yd_ref, x_ref, wg_ref, bg_ref, wd_ref, wo_ref, gn_ref, *outs, last):
    h = h_ref[0]
    ys = (ya_ref[0], yb_ref[0], yc_ref[0], yd_ref[0])
    merged = None
    for kbr in range(4):
        lo, hi = kbr * D_MODEL, (kbr + 1) * D_MODEL
        gate = jax.nn.sigmoid(_dot(h, wg_ref[:, lo:hi]) + bg_ref[:, lo:hi])
        term = gate * _dot(ys[kbr], wd_ref[kbr])
        merged = term if merged is None else merged + term
    xn = x_ref[0] + _dot(merged.astype(BF16), wo_ref[...])
    if last:
        outs[0][0] = _rms(xn, gn_ref[...])
    else:
        outs[0][0] = xn
        outs[1][0] = _rms(xn, gn_ref[...]).astype(BF16)


def _merge_call(h, ya, yb, yc, yd, x, wE, b_gate, w_down, w_out, g_next, l, *, tm, last):
    B, T, _ = x.shape
    row = pl.BlockSpec((1, tm, D_MODEL), lambda b, t: (b, t, 0))
    wspec = lambda shape: _layer_spec(shape, l)
    if last:
        out_specs = [row]
        out_shape = [jax.ShapeDtypeStruct((B, T, D_MODEL), F32)]
    else:
        out_specs = [row, row]
        out_shape = [jax.ShapeDtypeStruct((B, T, D_MODEL), F32),
                     jax.ShapeDtypeStruct((B, T, D_MODEL), BF16)]
    return pl.pallas_call(
        functools.partial(_merge_body, last=last),
        grid=(B, T // tm),
        in_specs=[row, row, row, row, row, row,
                  wspec((D_MODEL, 4 * D_MODEL)), wspec((1, 4 * D_MODEL)),
                  wspec((4, BR_WIDTH, D_MODEL)), wspec((D_MODEL, D_MODEL)),
                  pl.BlockSpec((1, D_MODEL), lambda b, t: (0, 0))],
        out_specs=out_specs,
        out_shape=out_shape,
        compiler_params=pltpu.CompilerParams(dimension_semantics=("arbitrary", "arbitrary"),
                                             vmem_limit_bytes=VMEM_LIMIT),
        name="merge_out",
    )(h, ya, yb, yc, yd, x, wE, b_gate, w_down, w_out, g_next)


def _pack_params(w_in, b_gate, b_igate, b_fgate, ssd_conv_w, ssd_conv_b, ssd_dt_bias, ssd_a_log, ssd_d, ssd_norm,
                 s5_a_re, s5_a_im, s5_log_dt, s5_b_re, s5_b_im, s5_c_re, s5_c_im, s5_d, s5_glu_w, s5_glu_b,
                 ml_norm, w_down, w_out):
    offs = {}
    acc = 0
    for name, size in zip(IN_NAMES, IN_SIZES):
        offs[name] = (acc, acc + size)
        acc += size
    col = lambda name: w_in[:, :, offs[name][0]:offs[name][1]]
    padl = lambda a: jnp.pad(a, ((0, 0), (0, 0), (0, LANES - a.shape[-1])))
    p = {}
    p["wA"] = jnp.concatenate([col("xbc"), col("z_ssd"), padl(col("dt"))], axis=-1).astype(BF16)
    p["wstA"] = jnp.swapaxes(col("dt"), 1, 2).astype(BF16)
    p["wB"] = jnp.concatenate([col("u_s5"), col("z_s5")], axis=-1).astype(BF16)
    p["wC"] = jnp.concatenate([col("q"), col("k"), col("v"), padl(col("i")), padl(col("f")), col("o"),
                               col("z_ml")], axis=-1).astype(BF16)
    zrow = jnp.zeros((DEPTH, 4, D_MODEL), F32)
    p["wstC"] = jnp.concatenate([jnp.swapaxes(col("i"), 1, 2), zrow, jnp.swapaxes(col("f"), 1, 2), zrow],
                                axis=1).astype(BF16)
    p["wD"] = jnp.concatenate([col("q_xa"), col("z_xa")], axis=-1).astype(BF16)
    p["wE"] = col("gate").astype(BF16)
    p["b_gate"] = b_gate[:, None, :]
    p["w_down"] = w_down.astype(BF16)
    p["w_out"] = w_out.astype(BF16)

    padv = lambda a: jnp.pad(a, ((0, 0), (0, LANES - a.shape[-1])))
    zl = jnp.zeros((DEPTH, 6, LANES), F32)
    p["prowA"] = jnp.concatenate([padv(ssd_dt_bias)[:, None], padv(ssd_a_log)[:, None], zl], axis=1)
    p["pcolA"] = jnp.stack([ssd_dt_bias, ssd_a_log], axis=-1)
    p["conv_w"] = ssd_conv_w
    p["conv_b"] = ssd_conv_b[:, None, :]
    p["dskA"] = jnp.repeat(ssd_d, SSD_HEAD_DIM, axis=-1)[:, None, :]
    p["nrmA"] = ssd_norm[:, None, :]

    p["prowC"] = jnp.concatenate([padv(b_igate)[:, None], padv(b_fgate)[:, None], zl], axis=1)
    z4 = jnp.zeros((DEPTH, 4), F32)
    p["pcolC"] = jnp.concatenate([b_igate, z4, b_fgate, z4], axis=1)[:, :, None]
    p["nrmC"] = ml_norm.reshape(DEPTH, 1, BR_WIDTH)

    dt = jnp.exp(s5_log_dt)[:, :, None]
    mag = jnp.exp(s5_a_re * dt)
    lr = mag * jnp.cos(s5_a_im * dt)
    li = mag * jnp.sin(s5_a_im * dt)
    den = s5_a_re * s5_a_re + s5_a_im * s5_a_im
    cr = ((lr - 1.0) * s5_a_re + li * s5_a_im) / den
    ci = (li * s5_a_re - (lr - 1.0) * s5_a_im) / den
    bb_re = cr[..., None] * s5_b_re - ci[..., None] * s5_b_im
    bb_im = cr[..., None] * s5_b_im + ci[..., None] * s5_b_re
    eye = jnp.eye(S5_CLUSTER_GROUPS, dtype=F32)
    cg = S5_CLUSTER_GROUPS

    def pack_b(bb):
        bb = bb.reshape(DEPTH, S5_CLUSTERS, cg, S5_STATE, S5_GROUP)
        return jnp.einsum("ljgnc,gh->ljgchn", bb, eye).reshape(DEPTH, S5_CLUSTERS, cg * S5_GROUP, cg * S5_STATE)

    def pack_c(cc):
        cc = cc.reshape(DEPTH, S5_CLUSTERS, cg, S5_GROUP, S5_STATE)
        return jnp.einsum("ljgcn,gh->ljgnhc", cc, eye).reshape(DEPTH, S5_CLUSTERS, cg * S5_STATE, cg * S5_GROUP)

    p["bc"] = jnp.concatenate([pack_b(bb_re), pack_b(bb_im)], axis=-1).astype(BF16)
    p["cc"] = jnp.concatenate([pack_c(s5_c_re), pack_c(-s5_c_im)], axis=-2).astype(BF16)
    p["lam"] = jnp.stack([lr.reshape(DEPTH, S5_FLAT), li.reshape(DEPTH, S5_FLAT)], axis=1)
    p["dskB"] = s5_d[:, None, :]
    p["glu_w"] = s5_glu_w.astype(BF16)
    p["glu_b"] = s5_glu_b[:, None, :]
    return p


def _group_cfg(B, T):
    if T % 512 == 0:
        return dict(Tp=None, ssd=(1, 256, 256, 2), ml=(1, 256, 256, 2), xa=(1, 256), s5=(8, 32), tm=256)
    assert T == 4 and B % 32 == 0
    return dict(Tp=True, ssd=(8, 16, T, 1), ml=(4, 32, T, 1), xa=(4, 8), s5=(32, T), tm=B * T)


def _pad_time(h, Lp):
    return jnp.pad(h, ((0, 0), (0, Lp - h.shape[1]), (0, 0)))


def _layer(p, l, x, h, mk, mv, lkv, states, ls, prev, g_next, last):
    B, T, _ = x.shape
    cfg = _group_cfg(B, T)
    padded = cfg["Tp"] is not None
    conv0, ssd0, sre0, sim0, c0, n0, m0 = states
    prev = prev or dict(a=None, b=None, c=None)

    h2d = h.reshape(B * T, D_MODEL)
    pj = (lambda w, tn: _proj_call(h2d, w, l, tn).reshape(B, T, -1)) if padded else (lambda w, tn: None)

    nb, L, valid, nsub = cfg["ssd"]
    hin = _pad_time(h, L) if padded else h
    ya, st_a = _ssd_call(hin, p, conv0, ssd0, l, ls, prev["a"], nb=nb, L=L, valid=valid, nsub=nsub,
                         proj=pj(p["wA"], WA_COLS // 3))
    nb, L, valid, nsub = cfg["ml"]
    hin = _pad_time(h, L) if padded else h
    yc, st_c = _mlstm_call(hin, p, c0, n0, m0, l, ls, prev["c"], nb=nb, L=L, valid=valid, nsub=nsub)
    nb, L = cfg["xa"]
    if padded:
        yd = _xattn_rows_call(pj(p["wD"], WD_COLS // 2), mk, mv, lkv, nb=nb, L=L)
    else:
        yd = _xattn_call(h, p["wD"], mk, mv, l, lkv, nb=nb, L=L)
    nb, Lt = cfg["s5"]
    yb, st_b = _s5_call(h, p, sre0, sim0, l, ls, prev["b"], nb=nb, Lt=Lt)
    new_states = dict(a=st_a, b=st_b, c=st_c)

    flat = (lambda a: a[:, :T].reshape(1, B * T, D_MODEL)) if padded else (lambda a: a)
    outs = _merge_call(flat(h), flat(ya), flat(yb), flat(yc), flat(yd), flat(x), p["wE"], p["b_gate"], p["w_down"],
                       p["w_out"], g_next, l, tm=cfg["tm"], last=last)
    x_new = outs[0].reshape(B, T, D_MODEL)
    h_next = None if last else outs[1].reshape(B, T, D_MODEL)
    return (x_new, h_next), new_states


def _first_norm(x, g):
    B, T, _ = x.shape
    cfg = _group_cfg(B, T)
    if cfg["Tp"] is None:
        return _norm_call(x, g, cfg["tm"])
    return _norm_call(x.reshape(1, B * T, D_MODEL), g, B * T).reshape(B, T, D_MODEL)


def _empty_states(B):
    z = lambda *s: jnp.zeros((DEPTH,) + s, F32)
    return dict(a=(z(B, SSD_CONV - 1, SSD_CONV_DIM), z(B, SSD_HEADS, SSD_HEAD_DIM, SSD_STATE)),
                b=(z(B * S5_GROUPS, S5_STATE), z(B * S5_GROUPS, S5_STATE)),
                c=(z(B, ML_HEADS, ML_HEAD_DIM, ML_HEAD_DIM), z(B, ML_HEADS, ML_HEAD_DIM), z(B, 1, LANES)))


def kernel(x_prompt, x_sample, mem_prompt, cache_mem_k, cache_mem_v, state_ssd_conv, state_ssd, state_s5_re,
           state_s5_im, state_mlstm_c, state_mlstm_n, state_mlstm_m, norm_in, w_in, b_gate, b_igate, b_fgate,
           ssd_conv_w, ssd_conv_b, ssd_dt_bias, ssd_a_log, ssd_d, ssd_norm, s5_a_re, s5_a_im, s5_log_dt, s5_b_re,
           s5_b_im, s5_c_re, s5_c_im, s5_d, s5_glu_w, s5_glu_b, ml_norm, mem_norm, w_mem_kv, w_down, w_out,
           final_norm):
    p = _pack_params(w_in, b_gate, b_igate, b_fgate, ssd_conv_w, ssd_conv_b, ssd_dt_bias, ssd_a_log, ssd_d,
                     ssd_norm, s5_a_re, s5_a_im, s5_log_dt, s5_b_re, s5_b_im, s5_c_re, s5_c_im, s5_d, s5_glu_w,
                     s5_glu_b, ml_norm, w_down, w_out)
    Bp, Tp, _ = x_prompt.shape
    Bs, Ts, _ = x_sample.shape

    mk_p, mv_p, mk_out, mv_out = _memkv_call(mem_prompt, mem_norm[:, None, :], w_mem_kv.astype(BF16))

    zeros = lambda *s: jnp.zeros((1,) + s, F32)
    states_p = (zeros(Bp, SSD_CONV - 1, SSD_CONV_DIM), zeros(Bp, SSD_HEADS, SSD_HEAD_DIM, SSD_STATE),
                zeros(Bp * S5_GROUPS, S5_STATE), zeros(Bp * S5_GROUPS, S5_STATE),
                zeros(Bp, ML_HEADS, ML_HEAD_DIM, ML_HEAD_DIM), zeros(Bp, ML_HEADS, ML_HEAD_DIM),
                zeros(Bp, 1, LANES))
    m_pad = jnp.pad(state_mlstm_m, ((0, 0), (0, 0), (0, LANES - ML_HEADS)))[:, :, None, :]
    states_s = (state_ssd_conv, state_ssd, state_s5_re.reshape(DEPTH, Bs * S5_GROUPS, S5_STATE),
                state_s5_im.reshape(DEPTH, Bs * S5_GROUPS, S5_STATE), state_mlstm_c, state_mlstm_n, m_pad)

    g0 = norm_in[0][None, :]
    xp, hp = x_prompt, _first_norm(x_prompt, g0)
    xs, hs = x_sample, _first_norm(x_sample, g0)
    st_p, st_s = _empty_states(Bp), _empty_states(Bs)
    for l in range(DEPTH):
        last = l == DEPTH - 1
        g_next = final_norm[None, :] if last else norm_in[l + 1][None, :]
        (xp, hp), st_p = _layer(p, l, xp, hp, mk_p, mv_p, l, states_p, 0, st_p, g_next, last)
        (xs, hs), st_s = _layer(p, l, xs, hs, cache_mem_k, cache_mem_v, l, states_s, l, st_s, g_next, last)

    def unpack(st, B):
        conv, ssd = st["a"]
        sre, sim = st["b"]
        c, n, m = st["c"]
        return (conv, ssd, sre.reshape(DEPTH, B, S5_GROUPS, S5_STATE), sim.reshape(DEPTH, B, S5_GROUPS, S5_STATE),
                c, n, m[:, :, 0, :ML_HEADS])

    return (xp, xs, mk_out, mv_out) + unpack(st_p, Bp) + unpack(st_s, Bs)
```

```python
import functools
import math

import jax
import jax.numpy as jnp
from jax import lax
from jax.experimental import pallas as pl
from jax.experimental.pallas import tpu as pltpu

F32 = jnp.float32
BF16 = jnp.bfloat16
NEG_INF = float("-inf")

D_MODEL = 1024
DEPTH = 4
BR_WIDTH = D_MODEL
SSD_HEADS = 16
SSD_HEAD_DIM = 64
SSD_GROUPS = 2
SSD_STATE = 128
SSD_CONV = 4
SSD_CONV_DIM = BR_WIDTH + 2 * SSD_GROUPS * SSD_STATE
S5_GROUP = 16
S5_GROUPS = 64
S5_STATE = 64
S5_CLUSTERS = 4
S5_CLUSTER_GROUPS = S5_GROUPS // S5_CLUSTERS
S5_FLAT = S5_GROUPS * S5_STATE
ML_HEADS = 4
ML_HEAD_DIM = 256
MEM_TOKENS = 256
XA_HEADS = 4
XA_HEAD_DIM = 256
EPS = 1e-6
LANES = 128
CONV_PAD = 8

IN_NAMES = ("z_ssd", "xbc", "dt", "u_s5", "z_s5", "q", "k", "v", "i", "f", "o", "z_ml", "q_xa", "z_xa", "gate")
IN_SIZES = (BR_WIDTH, SSD_CONV_DIM, SSD_HEADS, BR_WIDTH, BR_WIDTH, BR_WIDTH, BR_WIDTH, BR_WIDTH,
            ML_HEADS, ML_HEADS, BR_WIDTH, BR_WIDTH, BR_WIDTH, BR_WIDTH, 4 * D_MODEL)

WA_COLS = SSD_CONV_DIM + BR_WIDTH + LANES
WB_COLS = 2 * BR_WIDTH
WC_COLS = 3 * BR_WIDTH + 2 * LANES + 2 * BR_WIDTH
WD_COLS = 2 * BR_WIDTH

VMEM_LIMIT = 56 * 1024 * 1024


def _dot(a, b):
    return jnp.dot(a, b, preferred_element_type=F32)


def _dot_nt(a, b):
    return lax.dot_general(a, b, (((1,), (1,)), ((), ())), preferred_element_type=F32)


def _split3(x):
    hi = x.astype(BF16)
    r = x - hi.astype(F32)
    mid = r.astype(BF16)
    lo = (r - mid.astype(F32)).astype(BF16)
    return hi, mid, lo


def _sel_left(sel, x):
    hi, mid, lo = _split3(x)
    return _dot(sel, hi) + _dot(sel, mid) + _dot(sel, lo)


def _sel_right(x, sel):
    hi, mid, lo = _split3(x)
    return _dot(hi, sel) + _dot(mid, sel) + _dot(lo, sel)


def _softplus(x):
    return jnp.maximum(x, 0.0) + jnp.log1p(jnp.exp(-jnp.abs(x)))


def _silu(x):
    return x * jax.nn.sigmoid(x)


def _rms(x, g):
    return x * lax.rsqrt(jnp.mean(x * x, axis=-1, keepdims=True) + EPS) * g


def _seq_masks(nb, L):
    M = nb * L
    ri = lax.broadcasted_iota(jnp.int32, (M, M), 0)
    ci = lax.broadcasted_iota(jnp.int32, (M, M), 1)
    if nb > 1:
        sh = int(math.log2(L))
        same = lax.shift_right_logical(ri, sh) == lax.shift_right_logical(ci, sh)
        causal = same & (ci <= ri)
        upper = same & (ri <= ci)
    else:
        same = None
        causal = ci <= ri
        upper = ri <= ci
    return same, causal, upper


def _valid_masks(nb, L, valid):
    M = nb * L
    rv = (lax.broadcasted_iota(jnp.int32, (M, 1), 0) & (L - 1)) < valid
    cv = (lax.broadcasted_iota(jnp.int32, (1, M), 1) & (L - 1)) < valid
    return rv, cv


def _row_sel(nb, L, b, dtype):
    rows = lax.broadcasted_iota(jnp.int32, (nb * L, 1), 0)
    return (lax.shift_right_logical(rows, int(math.log2(L))) == b).astype(dtype)


def _col_pieces(n_cols, width):
    return [(c0, min(c0 + width, n_cols)) for c0 in range(0, n_cols, width)]


class _Spread:
    def __init__(self, thunks, n_slots):
        self.thunks, self.n_slots, self.calls, self.done = list(thunks), n_slots, 0, 0

    def slot(self):
        self.calls += 1
        while self.done < len(self.thunks) and self.done * self.n_slots < self.calls * len(self.thunks):
            self.thunks[self.done]()
            self.done += 1

    def flush(self):
        while self.done < len(self.thunks):
            self.thunks[self.done]()
            self.done += 1


def _proj_piece(proj_scr, slot, hb, w_ref, c0, c1):
    proj_scr[slot, :, c0:c1] = _dot(hb, w_ref[:, c0:c1])


def _next_chunk_spec(nb, L, nsub, n_steps, n_blocks):
    def index(i, c):
        nxt = jnp.minimum(i * n_steps + c + 1, n_blocks * n_steps - 1)
        return (nxt // n_steps, (nxt % n_steps) * nsub, 0)
    return pl.BlockSpec((nb, L, D_MODEL), index)


def _fill_padded_proj(proj_scr, p_ref, nb, L, valid):
    @pl.when((pl.program_id(0) == 0) & (pl.program_id(1) == 0))
    def _zero():
        proj_scr[0] = jnp.zeros(proj_scr.shape[1:], F32)

    for b in range(nb):
        proj_scr[0, b * L:b * L + valid, :] = p_ref[b]


def _proj_body(h_ref, w_ref, o_ref):
    o_ref[...] = _dot(h_ref[...], w_ref[...])


def _proj_call(h2d, w, l, tn):
    R = h2d.shape[0]
    cols = w.shape[-1]
    return pl.pallas_call(
        _proj_body,
        grid=(cols // tn,),
        in_specs=[pl.BlockSpec((R, D_MODEL), lambda j: (0, 0)),
                  pl.BlockSpec((None, D_MODEL, tn), lambda j: (l, 0, j))],
        out_specs=pl.BlockSpec((R, tn), lambda j: (0, j)),
        out_shape=jax.ShapeDtypeStruct((R, cols), F32),
        compiler_params=pltpu.CompilerParams(dimension_semantics=("arbitrary",), vmem_limit_bytes=VMEM_LIMIT),
        name="in_proj",
    )(h2d, w)


def _layer_spec(shape, l):
    return pl.BlockSpec((None,) + shape, lambda *_: (l,) + (0,) * len(shape), pipeline_mode=pl.Buffered(1))


def _state_spec(shape, l):
    return pl.BlockSpec((None,) + shape, lambda i, c: (l, i) + (0,) * (len(shape) - 1))


def _branch_call(body, *, name, grid, in_specs, args, y_spec, y_shape, st_specs, st_shapes, prev, scratch):
    n_in = len(args)
    prev = () if prev is None else tuple(prev)

    def wrapped(*refs):
        body(*refs[:n_in], *refs[n_in + len(prev):])

    outs = pl.pallas_call(
        wrapped,
        grid=grid,
        in_specs=list(in_specs) + [pl.BlockSpec(memory_space=pl.ANY)] * len(prev),
        out_specs=[y_spec] + list(st_specs),
        out_shape=[y_shape] + list(st_shapes),
        scratch_shapes=scratch,
        input_output_aliases={n_in + k: 1 + k for k in range(len(prev))},
        compiler_params=pltpu.CompilerParams(dimension_semantics=("arbitrary", "arbitrary"),
                                             vmem_limit_bytes=VMEM_LIMIT),
        name=name,
    )(*args, *prev)
    return outs[0], tuple(outs[1:])


def _norm_body(x_ref, g_ref, h_ref):
    h_ref[0] = _rms(x_ref[0], g_ref[...]).astype(BF16)


def _norm_call(x, g, tm):
    B, T, _ = x.shape
    return pl.pallas_call(
        _norm_body,
        grid=(B, T // tm),
        in_specs=[pl.BlockSpec((1, tm, D_MODEL), lambda b, t: (b, t, 0)),
                  pl.BlockSpec((1, D_MODEL), lambda b, t: (0, 0))],
        out_specs=pl.BlockSpec((1, tm, D_MODEL), lambda b, t: (b, t, 0)),
        out_shape=jax.ShapeDtypeStruct((B, T, D_MODEL), BF16),
        compiler_params=pltpu.CompilerParams(dimension_semantics=("arbitrary", "arbitrary")),
        name="rmsnorm_in",
    )(x, g)


def _memkv_body(mem_ref, g_ref, w_ref, mk_ref, mv_ref, mk5_ref, mv5_ref):
    hb = _rms(mem_ref[0], g_ref[0]).astype(BF16)
    kv = _dot(hb, w_ref[0])
    mk_ref[0, 0] = kv[:, :BR_WIDTH]
    mv_ref[0, 0] = kv[:, BR_WIDTH:]
    for hd in range(XA_HEADS):
        lo, hi = hd * XA_HEAD_DIM, (hd + 1) * XA_HEAD_DIM
        mk5_ref[0, 0, :, hd, :] = kv[:, lo:hi]
        mv5_ref[0, 0, :, hd, :] = kv[:, BR_WIDTH + lo:BR_WIDTH + hi]


def _memkv_call(mem, g, w_kv):
    B = mem.shape[0]
    out = jax.ShapeDtypeStruct((DEPTH, B, MEM_TOKENS, BR_WIDTH), F32)
    out5 = jax.ShapeDtypeStruct((DEPTH, B, MEM_TOKENS, XA_HEADS, XA_HEAD_DIM), F32)
    spec5 = pl.BlockSpec((1, 1, MEM_TOKENS, XA_HEADS, XA_HEAD_DIM), lambda l, b: (l, b, 0, 0, 0))
    return pl.pallas_call(
        _memkv_body,
        grid=(DEPTH, B),
        in_specs=[pl.BlockSpec((1, MEM_TOKENS, D_MODEL), lambda l, b: (b, 0, 0)),
                  pl.BlockSpec((1, 1, D_MODEL), lambda l, b: (l, 0, 0)),
                  pl.BlockSpec((1, D_MODEL, 2 * BR_WIDTH), lambda l, b: (l, 0, 0))],
        out_specs=[pl.BlockSpec((1, 1, MEM_TOKENS, BR_WIDTH), lambda l, b: (l, b, 0, 0)),
                   pl.BlockSpec((1, 1, MEM_TOKENS, BR_WIDTH), lambda l, b: (l, b, 0, 0)), spec5, spec5],
        out_shape=[out, out, out5, out5],
        compiler_params=pltpu.CompilerParams(dimension_semantics=("arbitrary", "arbitrary"),
                                             vmem_limit_bytes=VMEM_LIMIT),
        name="mem_kv",
    )(mem, g, w_kv)


def _ssd_body(h_ref, hn_ref, w_ref, wst_ref, prow_ref, pcol_ref, cw_ref, cb_ref, dsk_ref, nrm_ref, buf0_ref,
              s0_ref, y_ref, bufo_ref, so_ref, xp_scr, yin_scr, yst_scr, proj_scr, *, nb, L, valid, nsub):
    M = nb * L
    pipelined = nsub % 2 == 0
    chunk = lambda ref, s: ref[:, s * L:(s + 1) * L, :].reshape(M, D_MODEL)

    @pl.when(pl.program_id(1) == 0)
    def _init():
        so_ref[...] = s0_ref[...]
        xp_scr[:, CONV_PAD - 3:CONV_PAD, :] = buf0_ref[...]

    if pipelined:
        @pl.when((pl.program_id(0) == 0) & (pl.program_id(1) == 0))
        def _prologue():
            proj_scr[0] = _dot(chunk(h_ref, 0), w_ref[...])
    else:
        assert nsub == 1
        _fill_padded_proj(proj_scr, w_ref, nb, L, valid)

    for s in range(nsub):
        hb_next = chunk(h_ref, s + 1) if s + 1 < nsub else (hn_ref[...].reshape(M, D_MODEL) if pipelined else None)
        thunks = [] if hb_next is None else [
            functools.partial(_proj_piece, proj_scr, (s + 1) % 2, hb_next, w_ref, c0, c1)
            for c0, c1 in _col_pieces(WA_COLS, 512)]
        _ssd_chunk(chunk(h_ref, s), proj_scr.at[s % 2], s, _Spread(thunks, SSD_HEADS * (1 + nb)),
                   wst_ref, prow_ref, pcol_ref, cw_ref, cb_ref, dsk_ref, nrm_ref,
                   y_ref, bufo_ref, so_ref, xp_scr, yin_scr, yst_scr, nb=nb, L=L, valid=valid)


def _ssd_chunk(hb, proj, s, spread, wst_ref, prow_ref, pcol_ref, cw_ref, cb_ref, dsk_ref, nrm_ref,
               y_ref, bufo_ref, so_ref, xp_scr, yin_scr, yst_scr, *, nb, L, valid):
    M = nb * L
    hg = SSD_HEADS // SSD_GROUPS

    xp_scr[:, CONV_PAD:CONV_PAD + L, :] = proj[:, :SSD_CONV_DIM].reshape(nb, L, SSD_CONV_DIM)
    conv = cb_ref[...][None]
    for k in range(SSD_CONV):
        o = CONV_PAD - 3 + k
        conv = conv + xp_scr[:, o:o + L, :] * cw_ref[k:k + 1, :][None]
    nbuf = xp_scr[:, CONV_PAD - 3 + valid:CONV_PAD + valid, :]
    xp_scr[:, CONV_PAD - 3:CONV_PAD, :] = nbuf
    bufo_ref[...] = nbuf
    xbc = _silu(conv).reshape(M, SSD_CONV_DIM)
    xs = xbc[:, :BR_WIDTH]
    bm = xbc[:, BR_WIDTH:BR_WIDTH + SSD_GROUPS * SSD_STATE].astype(BF16)
    cm = xbc[:, BR_WIDTH + SSD_GROUPS * SSD_STATE:].astype(BF16)
    z = proj[:, SSD_CONV_DIM:SSD_CONV_DIM + BR_WIDTH]

    same, causal, upper = _seq_masks(nb, L)
    tril_b = causal.astype(BF16)
    triu_b = upper.astype(BF16)
    same_b = jnp.ones((M, M), BF16) if same is None else same.astype(BF16)

    dt = _softplus(proj[:, SSD_CONV_DIM + BR_WIDTH:] + prow_ref[0:1, :])
    dtT = _softplus(_dot_nt(wst_ref[...], hb) + pcol_ref[:, 0:1])
    dt = jnp.where(lax.broadcasted_iota(jnp.int32, (1, LANES), 1) < SSD_HEADS, dt, 0.0)
    if valid < L:
        rv, cv = _valid_masks(nb, L, valid)
        dt = jnp.where(rv, dt, 0.0)
        dtT = jnp.where(cv, dtT, 0.0)
    la = dt * (-jnp.exp(prow_ref[1:2, :]))
    laT = dtT * (-jnp.exp(pcol_ref[:, 1:2]))
    acs = _sel_left(tril_b, la)
    tot = _sel_left(same_b, la)
    acsT = _sel_right(laT, triu_b)

    totT = _sel_right(laT, same_b)
    tailT = jnp.exp(totT - acsT)
    eend = jnp.exp(tot)
    expand = (lax.shift_right_logical(lax.broadcasted_iota(jnp.int32, (LANES, BR_WIDTH), 1), 6)
              == lax.broadcasted_iota(jnp.int32, (LANES, BR_WIDTH), 0)).astype(BF16)
    eacs_x = _sel_right(jnp.exp(acs), expand)

    xsT = xs.T
    wT = dtT * tailT

    for g in range(SSD_GROUPS):
        cm_g = cm[:, g * SSD_STATE:(g + 1) * SSD_STATE]
        bm_g = bm[:, g * SSD_STATE:(g + 1) * SSD_STATE]
        cb = _dot_nt(cm_g, bm_g)
        ats = []
        for hh in range(hg):
            h = g * hg + hh
            lo, hi = h * SSD_HEAD_DIM, (h + 1) * SSD_HEAD_DIM
            seg = acs[:, h:h + 1] - acsT[h:h + 1, :]
            decay = jnp.exp(jnp.where(causal, seg, NEG_INF))
            xdtT = (xsT[lo:hi, :] * dtT[h:h + 1, :]).astype(BF16)
            ats.append((xsT[lo:hi, :] * wT[h:h + 1, :]).astype(BF16))
            yin_scr[:, lo:hi] = _dot_nt((cb * decay).astype(BF16), xdtT)
            spread.slot()
        at_g = jnp.concatenate(ats, axis=0)
        glo, ghi = g * hg, (g + 1) * hg
        for b in range(nb):
            r0, r1 = b * L, (b + 1) * L
            bm_b = bm_g if nb == 1 else bm_g * _row_sel(nb, L, b, BF16)
            s_old = so_ref[b, glo:ghi].reshape(hg * SSD_HEAD_DIM, SSD_STATE)
            yst_scr[r0:r1, glo * SSD_HEAD_DIM:ghi * SSD_HEAD_DIM] = _dot_nt(cm_g[r0:r1, :], s_old.astype(BF16))
            kept = jnp.concatenate(
                [eend[r0:r0 + 1, glo + hh:glo + hh + 1] * s_old[hh * SSD_HEAD_DIM:(hh + 1) * SSD_HEAD_DIM]
                 for hh in range(hg)], axis=0)
            so_ref[b, glo:ghi] = (kept + _dot(at_g, bm_b)).reshape(hg, SSD_HEAD_DIM, SSD_STATE)
            for _ in range(hg):
                spread.slot()
    spread.flush()

    y = yin_scr[...] + eacs_x * yst_scr[...] + dsk_ref[...] * xs
    y_ref[:, s * L:(s + 1) * L, :] = _rms(y * _silu(z), nrm_ref[...]).reshape(nb, L, BR_WIDTH).astype(BF16)


def _ssd_call(h, p, buf0, s0, l, ls, prev, *, nb, L, valid, nsub, proj=None):
    B, T, _ = h.shape
    wspec = lambda shape: _layer_spec(shape, l)
    conv_blk = (nb, SSD_CONV - 1, SSD_CONV_DIM)
    ssd_blk = (nb, SSD_HEADS, SSD_HEAD_DIM, SSD_STATE)
    w_arg, w_spec = ((p["wA"], wspec((D_MODEL, WA_COLS))) if proj is None else
                     (proj, pl.BlockSpec((nb, valid, WA_COLS), lambda i, c: (i, 0, 0))))
    return _branch_call(
        functools.partial(_ssd_body, nb=nb, L=L, valid=valid, nsub=nsub),
        name="ssd_branch",
        grid=(B // nb, T // (nsub * L)),
        in_specs=[pl.BlockSpec((nb, nsub * L, D_MODEL), lambda i, c: (i, c, 0)),
                  _next_chunk_spec(nb, L, nsub, T // (nsub * L), B // nb),
                  w_spec, wspec((SSD_HEADS, D_MODEL)),
                  wspec((8, LANES)), wspec((SSD_HEADS, 2)),
                  wspec((SSD_CONV, SSD_CONV_DIM)), wspec((1, SSD_CONV_DIM)),
                  wspec((1, BR_WIDTH)), wspec((1, BR_WIDTH)),
                  _state_spec(conv_blk, ls), _state_spec(ssd_blk, ls)],
        args=(h, h, w_arg, p["wstA"], p["prowA"], p["pcolA"], p["conv_w"], p["conv_b"], p["dskA"], p["nrmA"],
              buf0, s0),
        y_spec=pl.BlockSpec((nb, nsub * L, BR_WIDTH), lambda i, c: (i, c, 0)),
        y_shape=jax.ShapeDtypeStruct((B, T, BR_WIDTH), BF16),
        st_specs=[_state_spec(conv_blk, l), _state_spec(ssd_blk, l)],
        st_shapes=[jax.ShapeDtypeStruct((DEPTH, B) + conv_blk[1:], F32),
                   jax.ShapeDtypeStruct((DEPTH, B) + ssd_blk[1:], F32)],
        prev=prev,
        scratch=[pltpu.VMEM((nb, L + CONV_PAD, SSD_CONV_DIM), F32),
                 pltpu.VMEM((nb * L, BR_WIDTH), F32),
                 pltpu.VMEM((nb * L, BR_WIDTH), F32),
                 pltpu.VMEM((2, nb * L, WA_COLS), F32)])


S5_SCAN_LANES = 512


def _s5_body(h_ref, w_ref, bc_ref, cc_ref, lam_ref, dsk_ref, gw_ref, gb_ref, sre0_ref, sim0_ref,
             y_ref, sre_ref, sim_ref, hs_scr, st_scr, *, nb, Lt):
    half = S5_CLUSTER_GROUPS * S5_STATE

    @pl.when(pl.program_id(1) == 0)
    def _load_state():
        for g in range(S5_GROUPS):
            st_scr[0, :, g * S5_STATE:(g + 1) * S5_STATE] = sre0_ref[pl.ds(g, nb, stride=S5_GROUPS), :]
            st_scr[1, :, g * S5_STATE:(g + 1) * S5_STATE] = sim0_ref[pl.ds(g, nb, stride=S5_GROUPS), :]

    hb = jnp.swapaxes(h_ref[...].astype(F32), 0, 1).reshape(nb * Lt, D_MODEL).astype(BF16)
    proj = _dot(hb, w_ref[...])
    u = proj[:, :BR_WIDTH]
    z = proj[:, BR_WIDTH:]
    ub = u.astype(BF16)
    gl = S5_CLUSTER_GROUPS * S5_GROUP
    def b_u(j):
        hs_scr[:, 2 * half * j:2 * half * (j + 1)] = _dot(ub[:, gl * j:gl * (j + 1)], bc_ref[j])

    ys = []
    b_u(0)
    for j in range(S5_CLUSTERS):
        if j + 1 < S5_CLUSTERS:
            b_u(j + 1)
        for q in range(half // S5_SCAN_LANES):
            cr = 2 * half * j + S5_SCAN_LANES * q
            ci = cr + half
            sc = half * j + S5_SCAN_LANES * q
            lr = jnp.broadcast_to(lam_ref[0:1, sc:sc + S5_SCAN_LANES], (8, S5_SCAN_LANES))
            li = jnp.broadcast_to(lam_ref[1:2, sc:sc + S5_SCAN_LANES], (8, S5_SCAN_LANES))
            for sg in range(nb // 8):
                hr = st_scr[0, 8 * sg:8 * sg + 8, sc:sc + S5_SCAN_LANES]
                hi = st_scr[1, 8 * sg:8 * sg + 8, sc:sc + S5_SCAN_LANES]
                for t in range(Lt):
                    r0 = t * nb + 8 * sg
                    nr = lr * hr - li * hi + hs_scr[r0:r0 + 8, cr:cr + S5_SCAN_LANES]
                    ni = lr * hi + li * hr + hs_scr[r0:r0 + 8, ci:ci + S5_SCAN_LANES]
                    hs_scr[r0:r0 + 8, cr:cr + S5_SCAN_LANES] = nr
                    hs_scr[r0:r0 + 8, ci:ci + S5_SCAN_LANES] = ni
                    hr, hi = nr, ni
                st_scr[0, 8 * sg:8 * sg + 8, sc:sc + S5_SCAN_LANES] = hr
                st_scr[1, 8 * sg:8 * sg + 8, sc:sc + S5_SCAN_LANES] = hi
        ys.append(_dot(hs_scr[:, 2 * half * j:2 * half * (j + 1)].astype(BF16), cc_ref[j]))

    y = jnp.concatenate(ys, axis=1) + dsk_ref[...] * u
    yb = jax.nn.gelu(y)
    glu = jax.nn.sigmoid(_dot(yb.astype(BF16), gw_ref[...]) + gb_ref[...])
    out = (yb * glu * _silu(z)).reshape(Lt, nb, BR_WIDTH)
    y_ref[...] = jnp.swapaxes(out, 0, 1).astype(BF16)

    @pl.when(pl.program_id(1) == pl.num_programs(1) - 1)
    def _store_state():
        for g in range(S5_GROUPS):
            sre_ref[pl.ds(g, nb, stride=S5_GROUPS), :] = st_scr[0, :, g * S5_STATE:(g + 1) * S5_STATE]
            sim_ref[pl.ds(g, nb, stride=S5_GROUPS), :] = st_scr[1, :, g * S5_STATE:(g + 1) * S5_STATE]


def _s5_call(h, p, sre0, sim0, l, ls, prev, *, nb, Lt):
    B, T, _ = h.shape
    wspec = lambda shape: _layer_spec(shape, l)
    st_blk = (nb * S5_GROUPS, S5_STATE)
    st_shape = jax.ShapeDtypeStruct((DEPTH, B * S5_GROUPS, S5_STATE), F32)
    return _branch_call(
        functools.partial(_s5_body, nb=nb, Lt=Lt),
        name="s5_branch",
        grid=(B // nb, T // Lt),
        in_specs=[pl.BlockSpec((nb, Lt, D_MODEL), lambda i, c: (i, c, 0)),
                  wspec((D_MODEL, WB_COLS)),
                  wspec((S5_CLUSTERS, S5_CLUSTER_GROUPS * S5_GROUP, 2 * S5_CLUSTER_GROUPS * S5_STATE)),
                  wspec((S5_CLUSTERS, 2 * S5_CLUSTER_GROUPS * S5_STATE, S5_CLUSTER_GROUPS * S5_GROUP)),
                  wspec((2, S5_FLAT)), wspec((1, BR_WIDTH)),
                  wspec((BR_WIDTH, BR_WIDTH)), wspec((1, BR_WIDTH)),
                  _state_spec(st_blk, ls), _state_spec(st_blk, ls)],
        args=(h, p["wB"], p["bc"], p["cc"], p["lam"], p["dskB"], p["glu_w"], p["glu_b"], sre0, sim0),
        y_spec=pl.BlockSpec((nb, Lt, BR_WIDTH), lambda i, c: (i, c, 0)),
        y_shape=jax.ShapeDtypeStruct((B, T, BR_WIDTH), BF16),
        st_specs=[_state_spec(st_blk, l), _state_spec(st_blk, l)],
        st_shapes=[st_shape, st_shape],
        prev=prev,
        scratch=[pltpu.VMEM((nb * Lt, 2 * S5_FLAT), F32), pltpu.VMEM((2, nb, S5_FLAT), F32)])


def _mlstm_body(h_ref, hn_ref, w_ref, wst_ref, prow_ref, pcol_ref, nrm_ref, c0_ref, n0_ref, m0_ref,
                y_ref, co_ref, no_ref, mo_ref, num_scr, proj_scr, *, nb, L, valid, nsub):
    M = nb * L
    pipelined = nsub % 2 == 0
    chunk = lambda ref, s: ref[:, s * L:(s + 1) * L, :].reshape(M, D_MODEL)

    @pl.when(pl.program_id(1) == 0)
    def _init():
        co_ref[...] = c0_ref[...]
        no_ref[...] = n0_ref[...]
        mo_ref[...] = m0_ref[...]

    if pipelined:
        @pl.when((pl.program_id(0) == 0) & (pl.program_id(1) == 0))
        def _prologue():
            proj_scr[0] = _dot(chunk(h_ref, 0), w_ref[...])
    else:
        assert nsub == 1
        proj_scr[0] = _dot(chunk(h_ref, 0), w_ref[...])

    for s in range(nsub):
        hb_next = chunk(h_ref, s + 1) if s + 1 < nsub else (hn_ref[...].reshape(M, D_MODEL) if pipelined else None)
        thunks = [] if hb_next is None else [
            functools.partial(_proj_piece, proj_scr, (s + 1) % 2, hb_next, w_ref, c0, c1)
            for c0, c1 in _col_pieces(WC_COLS, 768)]
        _mlstm_chunk(chunk(h_ref, s), proj_scr.at[s % 2], s, _Spread(thunks, ML_HEADS * (1 + nb)),
                     wst_ref, prow_ref, pcol_ref, nrm_ref, y_ref, co_ref, no_ref, mo_ref, num_scr,
                     nb=nb, L=L, valid=valid)


def _mlstm_chunk(hb, proj, s, spread, wst_ref, prow_ref, pcol_ref, nrm_ref, y_ref, co_ref, no_ref, mo_ref,
                 num_scr, *, nb, L, valid):
    M = nb * L
    W = BR_WIDTH
    q = proj[:, :W]
    k = proj[:, W:2 * W] * (ML_HEAD_DIM ** -0.5)
    v = proj[:, 2 * W:3 * W]
    ig = proj[:, 3 * W:3 * W + LANES] + prow_ref[0:1, :]
    logf = -_softplus(-(proj[:, 3 * W + LANES:3 * W + 2 * LANES] + prow_ref[1:2, :]))
    o = proj[:, 3 * W + 2 * LANES:4 * W + 2 * LANES]
    z = proj[:, 4 * W + 2 * LANES:]
    smallT = _dot_nt(wst_ref[...], hb) + pcol_ref[...]
    igT = smallT[0:8, :]
    logfT = -_softplus(-smallT[8:16, :])
    if valid < L:
        rv, cv = _valid_masks(nb, L, valid)
        ig = jnp.where(rv, ig, NEG_INF)
        logf = jnp.where(rv, logf, 0.0)
        igT = jnp.where(cv, igT, NEG_INF)
        logfT = jnp.where(cv, logfT, 0.0)

    same, causal, upper = _seq_masks(nb, L)
    tril_b = causal.astype(BF16)
    triu_b = upper.astype(BF16)
    same_b = jnp.ones((M, M), BF16) if same is None else same.astype(BF16)
    bcum = _sel_left(tril_b, logf)
    tot = _sel_left(same_b, logf)
    bcumT = _sel_right(logfT, triu_b)
    totT = _sel_right(logfT, same_b)
    m_rows = jnp.broadcast_to(mo_ref[...], (nb, L, LANES)).reshape(M, LANES)
    gcar = bcum + m_rows
    g_end = tot + m_rows
    d_end = tot - bcum + ig
    d_endT = totT - bcumT + igT

    qb = q.astype(BF16)
    kb = k.astype(BF16)
    vb = v.astype(BF16)
    lane = lax.broadcasted_iota(jnp.int32, (M, LANES), 1)
    m_new_all = jnp.zeros((M, LANES), F32)
    per_head = []
    wk_parts = []
    for hd in range(ML_HEADS):
        lo, hi = hd * ML_HEAD_DIM, (hd + 1) * ML_HEAD_DIM
        dmat = jnp.where(causal, bcum[:, hd:hd + 1] - bcumT[hd:hd + 1, :] + igT[hd:hd + 1, :], NEG_INF)
        m_l = jnp.maximum(gcar[:, hd:hd + 1], jnp.max(dmat, axis=1, keepdims=True))
        w_inter = jnp.exp(gcar[:, hd:hd + 1] - m_l)
        qk = _dot_nt(qb[:, lo:hi], kb[:, lo:hi]) * jnp.exp(dmat - m_l)
        num_scr[:, lo:hi] = _dot(qk.astype(BF16), vb[:, lo:hi])
        den_intra = jnp.sum(qk, axis=1, keepdims=True)
        d_row = d_endT[hd:hd + 1, :] if same is None else jnp.where(same, d_endT[hd:hd + 1, :], NEG_INF)
        m_new = jnp.maximum(g_end[:, hd:hd + 1], jnp.max(d_row, axis=1, keepdims=True))
        w_end = jnp.exp(d_end[:, hd:hd + 1] - m_new)
        dec = jnp.exp(g_end[:, hd:hd + 1] - m_new)
        wk_parts.append(w_end * k[:, lo:hi])
        m_new_all = jnp.where(lane == hd, m_new, m_new_all)
        per_head.append((m_l, w_inter, den_intra, dec))
        spread.slot()
    wk = jnp.concatenate(wk_parts, axis=1)
    kt = wk.T.astype(BF16)

    for b in range(nb):
        r0, r1 = b * L, (b + 1) * L
        sel = None if nb == 1 else _row_sel(nb, L, b, BF16)
        for hd in range(ML_HEADS):
            lo, hi = hd * ML_HEAD_DIM, (hd + 1) * ML_HEAD_DIM
            m_l, w_inter, den_intra, dec = per_head[hd]
            c_old = co_ref[b, hd]
            n_old = no_ref[b, hd:hd + 1, :]
            q_r = q[r0:r1, lo:hi]
            wi = w_inter[r0:r1]
            num = num_scr[r0:r1, lo:hi] + wi * _dot(qb[r0:r1, lo:hi], c_old.astype(BF16))
            den = den_intra[r0:r1] + wi * jnp.sum(q_r * n_old, axis=1, keepdims=True)
            hout = num / jnp.maximum(jnp.abs(den), jnp.exp(-m_l[r0:r1]))
            ho = jax.nn.sigmoid(o[r0:r1, lo:hi]) * hout
            yv = _rms(ho, nrm_ref[:, lo:hi]) * _silu(z[r0:r1, lo:hi])
            y_ref[b, s * L:(s + 1) * L, lo:hi] = yv.astype(BF16)
            dec_b = dec[r0:r0 + 1]
            vb_b = vb[:, lo:hi] if sel is None else vb[:, lo:hi] * sel
            co_ref[b, hd] = dec_b * c_old + _dot(kt[lo:hi, :], vb_b)
            no_ref[b, hd:hd + 1, :] = dec_b * n_old + jnp.sum(wk[r0:r1, lo:hi], axis=0, keepdims=True)
            spread.slot()
        mo_ref[b] = m_new_all[r0:r0 + 1, :]
    spread.flush()


def _mlstm_call(h, p, c0, n0, m0, l, ls, prev, *, nb, L, valid, nsub):
    B, T, _ = h.shape
    wspec = lambda shape: _layer_spec(shape, l)
    w_arg, w_spec = p["wC"], wspec((D_MODEL, WC_COLS))
    c_blk = (nb, ML_HEADS, ML_HEAD_DIM, ML_HEAD_DIM)
    n_blk = (nb, ML_HEADS, ML_HEAD_DIM)
    m_blk = (nb, 1, LANES)
    return _branch_call(
        functools.partial(_mlstm_body, nb=nb, L=L, valid=valid, nsub=nsub),
        name="mlstm_branch",
        grid=(B // nb, T // (nsub * L)),
        in_specs=[pl.BlockSpec((nb, nsub * L, D_MODEL), lambda i, c: (i, c, 0)),
                  _next_chunk_spec(nb, L, nsub, T // (nsub * L), B // nb),
                  w_spec, wspec((16, D_MODEL)),
                  wspec((8, LANES)), wspec((16, 1)), wspec((1, BR_WIDTH)),
                  _state_spec(c_blk, ls), _state_spec(n_blk, ls), _state_spec(m_blk, ls)],
        args=(h, h, w_arg, p["wstC"], p["prowC"], p["pcolC"], p["nrmC"], c0, n0, m0),
        y_spec=pl.BlockSpec((nb, nsub * L, BR_WIDTH), lambda i, c: (i, c, 0)),
        y_shape=jax.ShapeDtypeStruct((B, T, BR_WIDTH), BF16),
        st_specs=[_state_spec(c_blk, l), _state_spec(n_blk, l), _state_spec(m_blk, l)],
        st_shapes=[jax.ShapeDtypeStruct((DEPTH, B) + c_blk[1:], F32),
                   jax.ShapeDtypeStruct((DEPTH, B) + n_blk[1:], F32),
                   jax.ShapeDtypeStruct((DEPTH, B) + m_blk[1:], F32)],
        prev=prev,
        scratch=[pltpu.VMEM((nb * L, BR_WIDTH), F32), pltpu.VMEM((2, nb * L, WC_COLS), F32)])


def _xattn_body(h_ref, hn_ref, w_ref, mk_ref, mv_ref, y_ref, proj_scr, *, L, nsub):
    @pl.when((pl.program_id(0) == 0) & (pl.program_id(1) == 0))
    def _prologue():
        proj_scr[0] = _dot(h_ref[0, 0:L, :], w_ref[...])

    for s in range(nsub):
        hb_next = h_ref[0, (s + 1) * L:(s + 2) * L, :] if s + 1 < nsub else hn_ref[0]
        spread = _Spread([functools.partial(_proj_piece, proj_scr, (s + 1) % 2, hb_next, w_ref, c0, c1)
                          for c0, c1 in _col_pieces(WD_COLS, 512)], XA_HEADS)
        proj = proj_scr.at[s % 2]
        for hd in range(XA_HEADS):
            lo, hi = hd * XA_HEAD_DIM, (hd + 1) * XA_HEAD_DIM
            sc = _dot_nt(proj[:, lo:hi].astype(BF16), mk_ref[0, :, lo:hi].astype(BF16)) * (XA_HEAD_DIM ** -0.5)
            e = jnp.exp(sc - jnp.max(sc, axis=1, keepdims=True))
            p = e / jnp.sum(e, axis=1, keepdims=True)
            a = _dot(p.astype(BF16), mv_ref[0, :, lo:hi].astype(BF16))
            y_ref[0, s * L:(s + 1) * L, lo:hi] = (a * _silu(proj[:, BR_WIDTH + lo:BR_WIDTH + hi])).astype(BF16)
            spread.slot()
        spread.flush()


def _xattn_rows_body(p_ref, mk_ref, mv_ref, y_ref, pj_scr, *, nb, L, valid):
    R = MEM_TOKENS * XA_HEADS
    Q = XA_HEADS * L

    @pl.when(pl.program_id(0) == 0)
    def _zero():
        pj_scr[...] = jnp.zeros(pj_scr.shape, F32)

    for b in range(nb):
        pj_scr[b * L:b * L + valid, :] = p_ref[b]
    q = pj_scr[:, :BR_WIDTH]
    z = pj_scr[:, BR_WIDTH:]
    row_head = lax.shift_right_logical(lax.broadcasted_iota(jnp.int32, (Q, R), 0), int(math.log2(L)))
    col_head = lax.broadcasted_iota(jnp.int32, (Q, R), 1) & (XA_HEADS - 1)
    own = row_head == col_head
    for b in range(nb):
        r0, r1 = b * L, (b + 1) * L
        qall = jnp.concatenate([q[r0:r1, hd * XA_HEAD_DIM:(hd + 1) * XA_HEAD_DIM] for hd in range(XA_HEADS)],
                               axis=0).astype(BF16)
        mk2 = mk_ref[b].reshape(R, XA_HEAD_DIM).astype(BF16)
        mv2 = mv_ref[b].reshape(R, XA_HEAD_DIM).astype(BF16)
        s = jnp.where(own, _dot_nt(qall, mk2) * (XA_HEAD_DIM ** -0.5), -1e30)
        e = jnp.exp(s - jnp.max(s, axis=1, keepdims=True))
        pr = e / jnp.sum(e, axis=1, keepdims=True)
        a = _dot(pr.astype(BF16), mv2)
        for hd in range(XA_HEADS):
            lo, hi = hd * XA_HEAD_DIM, (hd + 1) * XA_HEAD_DIM
            yv = a[hd * L:(hd + 1) * L, :] * _silu(z[r0:r1, lo:hi])
            y_ref[b, :, lo:hi] = yv[:valid].astype(BF16)


def _xattn_rows_call(proj, mk, mv, lkv, *, nb, L):
    B, T, _ = proj.shape
    assert T <= L
    kv_spec = pl.BlockSpec((None, nb, MEM_TOKENS, XA_HEADS, XA_HEAD_DIM), lambda i: (lkv, i, 0, 0, 0))
    return pl.pallas_call(
        functools.partial(_xattn_rows_body, nb=nb, L=L, valid=T),
        grid=(B // nb,),
        in_specs=[pl.BlockSpec((nb, T, WD_COLS), lambda i: (i, 0, 0)), kv_spec, kv_spec],
        out_specs=pl.BlockSpec((nb, T, BR_WIDTH), lambda i: (i, 0, 0)),
        out_shape=jax.ShapeDtypeStruct((B, T, BR_WIDTH), BF16),
        scratch_shapes=[pltpu.VMEM((nb * L, WD_COLS), F32)],
        compiler_params=pltpu.CompilerParams(dimension_semantics=("arbitrary",), vmem_limit_bytes=VMEM_LIMIT),
        name="xattn_rows",
    )(proj, mk, mv)


def _xattn_call(h, wD, mk, mv, l, lkv, *, L, nsub):
    B, T, _ = h.shape
    assert nsub % 2 == 0
    n_steps = T // (nsub * L)
    kv_spec = pl.BlockSpec((None, 1, MEM_TOKENS, BR_WIDTH), lambda i, c: (lkv, i, 0, 0))
    return pl.pallas_call(
        functools.partial(_xattn_body, L=L, nsub=nsub),
        grid=(B, n_steps),
        in_specs=[pl.BlockSpec((1, nsub * L, D_MODEL), lambda i, c: (i, c, 0)),
                  _next_chunk_spec(1, L, nsub, n_steps, B),
                  _layer_spec((D_MODEL, WD_COLS), l),
                  kv_spec, kv_spec],
        out_specs=pl.BlockSpec((1, nsub * L, BR_WIDTH), lambda i, c: (i, c, 0)),
        out_shape=jax.ShapeDtypeStruct((B, T, BR_WIDTH), BF16),
        scratch_shapes=[pltpu.VMEM((2, L, WD_COLS), F32)],
        compiler_params=pltpu.CompilerParams(dimension_semantics=("arbitrary", "arbitrary"),
                                             vmem_limit_bytes=VMEM_LIMIT),
        name="xattn_branch",
    )(h, h, wD, mk, mv)


def _merge_body(h_ref, ya_ref, yb_ref, yc_ref, yd_ref, x_ref, wg_ref, bg_ref, wd_ref, wo_ref, gn_ref, *outs, last):
    h = h_ref[0]
    ys = (ya_ref[0], yb_ref[0], yc_ref[0], yd_ref[0])
    merged = None
    for kbr in range(4):
        lo, hi = kbr * D_MODEL, (kbr + 1) * D_MODEL
        gate = jax.nn.sigmoid(_dot(h, wg_ref[:, lo:hi]) + bg_ref[:, lo:hi])
        term = gate * _dot(ys[kbr], wd_ref[kbr])
        merged = term if merged is None else merged + term
    xn = x_ref[0] + _dot(merged.astype(BF16), wo_ref[...])
    if last:
        outs[0][0] = _rms(xn, gn_ref[...])
    else:
        outs[0][0] = xn
        outs[1][0] = _rms(xn, gn_ref[...]).astype(BF16)


def _merge_call(h, ya, yb, yc, yd, x, wE, b_gate, w_down, w_out, g_next, l, *, tm, last):
    B, T, _ = x.shape
    row = pl.BlockSpec((1, tm, D_MODEL), lambda b, t: (b, t, 0))
    wspec = lambda shape: _layer_spec(shape, l)
    if last:
        out_specs = [row]
        out_shape = [jax.ShapeDtypeStruct((B, T, D_MODEL), F32)]
    else:
        out_specs = [row, row]
        out_shape = [jax.ShapeDtypeStruct((B, T, D_MODEL), F32),
                     jax.ShapeDtypeStruct((B, T, D_MODEL), BF16)]
    return pl.pallas_call(
        functools.partial(_merge_body, last=last),
        grid=(B, T // tm),
        in_specs=[row, row, row, row, row, row,
                  wspec((D_MODEL, 4 * D_MODEL)), wspec((1, 4 * D_MODEL)),
                  wspec((4, BR_WIDTH, D_MODEL)), wspec((D_MODEL, D_MODEL)),
                  pl.BlockSpec((1, D_MODEL), lambda b, t: (0, 0))],
        out_specs=out_specs,
        out_shape=out_shape,
        compiler_params=pltpu.CompilerParams(dimension_semantics=("arbitrary", "arbitrary"),
                                             vmem_limit_bytes=VMEM_LIMIT),
        name="merge_out",
    )(h, ya, yb, yc, yd, x, wE, b_gate, w_down, w_out, g_next)


def _pack_params(w_in, b_gate, b_igate, b_fgate, ssd_conv_w, ssd_conv_b, ssd_dt_bias, ssd_a_log, ssd_d, ssd_norm,
                 s5_a_re, s5_a_im, s5_log_dt, s5_b_re, s5_b_im, s5_c_re, s5_c_im, s5_d, s5_glu_w, s5_glu_b,
                 ml_norm, w_down, w_out):
    offs = {}
    acc = 0
    for name, size in zip(IN_NAMES, IN_SIZES):
        offs[name] = (acc, acc + size)
        acc += size
    col = lambda name: w_in[:, :, offs[name][0]:offs[name][1]]
    padl = lambda a: jnp.pad(a, ((0, 0), (0, 0), (0, LANES - a.shape[-1])))
    p = {}
    p["wA"] = jnp.concatenate([col("xbc"), col("z_ssd"), padl(col("dt"))], axis=-1).astype(BF16)
    p["wstA"] = jnp.swapaxes(col("dt"), 1, 2).astype(BF16)
    p["wB"] = jnp.concatenate([col("u_s5"), col("z_s5")], axis=-1).astype(BF16)
    p["wC"] = jnp.concatenate([col("q"), col("k"), col("v"), padl(col("i")), padl(col("f")), col("o"),
                               col("z_ml")], axis=-1).astype(BF16)
    zrow = jnp.zeros((DEPTH, 4, D_MODEL), F32)
    p["wstC"] = jnp.concatenate([jnp.swapaxes(col("i"), 1, 2), zrow, jnp.swapaxes(col("f"), 1, 2), zrow],
                                axis=1).astype(BF16)
    p["wD"] = jnp.concatenate([col("q_xa"), col("z_xa")], axis=-1).astype(BF16)
    p["wE"] = col("gate").astype(BF16)
    p["b_gate"] = b_gate[:, None, :]
    p["w_down"] = w_down.astype(BF16)
    p["w_out"] = w_out.astype(BF16)

    padv = lambda a: jnp.pad(a, ((0, 0), (0, LANES - a.shape[-1])))
    zl = jnp.zeros((DEPTH, 6, LANES), F32)
    p["prowA"] = jnp.concatenate([padv(ssd_dt_bias)[:, None], padv(ssd_a_log)[:, None], zl], axis=1)
    p["pcolA"] = jnp.stack([ssd_dt_bias, ssd_a_log], axis=-1)
    p["conv_w"] = ssd_conv_w
    p["conv_b"] = ssd_conv_b[:, None, :]
    p["dskA"] = jnp.repeat(ssd_d, SSD_HEAD_DIM, axis=-1)[:, None, :]
    p["nrmA"] = ssd_norm[:, None, :]

    p["prowC"] = jnp.concatenate([padv(b_igate)[:, None], padv(b_fgate)[:, None], zl], axis=1)
    z4 = jnp.zeros((DEPTH, 4), F32)
    p["pcolC"] = jnp.concatenate([b_igate, z4, b_fgate, z4], axis=1)[:, :, None]
    p["nrmC"] = ml_norm.reshape(DEPTH, 1, BR_WIDTH)

    dt = jnp.exp(s5_log_dt)[:, :, None]
    mag = jnp.exp(s5_a_re * dt)
    lr = mag * jnp.cos(s5_a_im * dt)
    li = mag * jnp.sin(s5_a_im * dt)
    den = s5_a_re * s5_a_re + s5_a_im * s5_a_im
    cr = ((lr - 1.0) * s5_a_re + li * s5_a_im) / den
    ci = (li * s5_a_re - (lr - 1.0) * s5_a_im) / den
    bb_re = cr[..., None] * s5_b_re - ci[..., None] * s5_b_im
    bb_im = cr[..., None] * s5_b_im + ci[..., None] * s5_b_re
    eye = jnp.eye(S5_CLUSTER_GROUPS, dtype=F32)
    cg = S5_CLUSTER_GROUPS

    def pack_b(bb):
        bb = bb.reshape(DEPTH, S5_CLUSTERS, cg, S5_STATE, S5_GROUP)
        return jnp.einsum("ljgnc,gh->ljgchn", bb, eye).reshape(DEPTH, S5_CLUSTERS, cg * S5_GROUP, cg * S5_STATE)

    def pack_c(cc):
        cc = cc.reshape(DEPTH, S5_CLUSTERS, cg, S5_GROUP, S5_STATE)
        return jnp.einsum("ljgcn,gh->ljgnhc", cc, eye).reshape(DEPTH, S5_CLUSTERS, cg * S5_STATE, cg * S5_GROUP)

    p["bc"] = jnp.concatenate([pack_b(bb_re), pack_b(bb_im)], axis=-1).astype(BF16)
    p["cc"] = jnp.concatenate([pack_c(s5_c_re), pack_c(-s5_c_im)], axis=-2).astype(BF16)
    p["lam"] = jnp.stack([lr.reshape(DEPTH, S5_FLAT), li.reshape(DEPTH, S5_FLAT)], axis=1)
    p["dskB"] = s5_d[:, None, :]
    p["glu_w"] = s5_glu_w.astype(BF16)
    p["glu_b"] = s5_glu_b[:, None, :]
    return p


def _group_cfg(B, T):
    if T % 512 == 0:
        return dict(Tp=None, ssd=(1, 256, 256, 2), ml=(1, 256, 256, 2), xa=(256, 2), s5=(8, 32), tm=256)
    assert T == 4 and B % 32 == 0
    return dict(Tp=True, ssd=(8, 16, T, 1), ml=(4, 32, T, 1), xa=(4, 8), s5=(32, T), tm=B * T)


def _pad_time(h, Lp):
    return jnp.pad(h, ((0, 0), (0, Lp - h.shape[1]), (0, 0)))


def _layer(p, l, x, h, mk, mv, lkv, states, ls, prev, g_next, last):
    B, T, _ = x.shape
    cfg = _group_cfg(B, T)
    padded = cfg["Tp"] is not None
    conv0, ssd0, sre0, sim0, c0, n0, m0 = states
    prev = prev or dict(a=None, b=None, c=None)

    h2d = h.reshape(B * T, D_MODEL)
    pj = (lambda w, tn: _proj_call(h2d, w, l, tn).reshape(B, T, -1)) if padded else (lambda w, tn: None)

    nb, L, valid, nsub = cfg["ssd"]
    hin = _pad_time(h, L) if padded else h
    ya, st_a = _ssd_call(hin, p, conv0, ssd0, l, ls, prev["a"], nb=nb, L=L, valid=valid, nsub=nsub,
                         proj=pj(p["wA"], WA_COLS // 3))
    nb, L, valid, nsub = cfg["ml"]
    hin = _pad_time(h, L) if padded else h
    yc, st_c = _mlstm_call(hin, p, c0, n0, m0, l, ls, prev["c"], nb=nb, L=L, valid=valid, nsub=nsub)
    if padded:
        nb, L = cfg["xa"]
        yd = _xattn_rows_call(pj(p["wD"], WD_COLS // 2), mk, mv, lkv, nb=nb, L=L)
    else:
        L, nsub = cfg["xa"]
        yd = _xattn_call(h, p["wD"], mk, mv, l, lkv, L=L, nsub=nsub)
    nb, Lt = cfg["s5"]
    yb, st_b = _s5_call(h, p, sre0, sim0, l, ls, prev["b"], nb=nb, Lt=Lt)
    new_states = dict(a=st_a, b=st_b, c=st_c)

    flat = (lambda a: a[:, :T].reshape(1, B * T, D_MODEL)) if padded else (lambda a: a)
    outs = _merge_call(flat(h), flat(ya), flat(yb), flat(yc), flat(yd), flat(x), p["wE"], p["b_gate"], p["w_down"],
                       p["w_out"], g_next, l, tm=cfg["tm"], last=last)
    x_new = outs[0].reshape(B, T, D_MODEL)
    h_next = None if last else outs[1].reshape(B, T, D_MODEL)
    return (x_new, h_next), new_states


def _first_norm(x, g):
    B, T, _ = x.shape
    cfg = _group_cfg(B, T)
    if cfg["Tp"] is None:
        return _norm_call(x, g, cfg["tm"])
    return _norm_call(x.reshape(1, B * T, D_MODEL), g, B * T).reshape(B, T, D_MODEL)


def _empty_states(B):
    z = lambda *s: jnp.zeros((DEPTH,) + s, F32)
    return dict(a=(z(B, SSD_CONV - 1, SSD_CONV_DIM), z(B, SSD_HEADS, SSD_HEAD_DIM, SSD_STATE)),
                b=(z(B * S5_GROUPS, S5_STATE), z(B * S5_GROUPS, S5_STATE)),
                c=(z(B, ML_HEADS, ML_HEAD_DIM, ML_HEAD_DIM), z(B, ML_HEADS, ML_HEAD_DIM), z(B, 1, LANES)))


def kernel(x_prompt, x_sample, mem_prompt, cache_mem_k, cache_mem_v, state_ssd_conv, state_ssd, state_s5_re,
           state_s5_im, state_mlstm_c, state_mlstm_n, state_mlstm_m, norm_in, w_in, b_gate, b_igate, b_fgate,
           ssd_conv_w, ssd_conv_b, ssd_dt_bias, ssd_a_log, ssd_d, ssd_norm, s5_a_re, s5_a_im, s5_log_dt, s5_b_re,
           s5_b_im, s5_c_re, s5_c_im, s5_d, s5_glu_w, s5_glu_b, ml_norm, mem_norm, w_mem_kv, w_down, w_out,
           final_norm):
    p = _pack_params(w_in, b_gate, b_igate, b_fgate, ssd_conv_w, ssd_conv_b, ssd_dt_bias, ssd_a_log, ssd_d,
                     ssd_norm, s5_a_re, s5_a_im, s5_log_dt, s5_b_re, s5_b_im, s5_c_re, s5_c_im, s5_d, s5_glu_w,
                     s5_glu_b, ml_norm, w_down, w_out)
    Bp, Tp, _ = x_prompt.shape
    Bs, Ts, _ = x_sample.shape

    mk_p, mv_p, mk_out, mv_out = _memkv_call(mem_prompt, mem_norm[:, None, :], w_mem_kv.astype(BF16))

    zeros = lambda *s: jnp.zeros((1,) + s, F32)
    states_p = (zeros(Bp, SSD_CONV - 1, SSD_CONV_DIM), zeros(Bp, SSD_HEADS, SSD_HEAD_DIM, SSD_STATE),
                zeros(Bp * S5_GROUPS, S5_STATE), zeros(Bp * S5_GROUPS, S5_STATE),
                zeros(Bp, ML_HEADS, ML_HEAD_DIM, ML_HEAD_DIM), zeros(Bp, ML_HEADS, ML_HEAD_DIM),
                zeros(Bp, 1, LANES))
    m_pad = jnp.pad(state_mlstm_m, ((0, 0), (0, 0), (0, LANES - ML_HEADS)))[:, :, None, :]
    states_s = (state_ssd_conv, state_ssd, state_s5_re.reshape(DEPTH, Bs * S5_GROUPS, S5_STATE),
                state_s5_im.reshape(DEPTH, Bs * S5_GROUPS, S5_STATE), state_mlstm_c, state_mlstm_n, m_pad)

    g0 = norm_in[0][None, :]
    xp, hp = x_prompt, _first_norm(x_prompt, g0)
    xs, hs = x_sample, _first_norm(x_sample, g0)
    st_p, st_s = _empty_states(Bp), _empty_states(Bs)
    for l in range(DEPTH):
        last = l == DEPTH - 1
        g_next = final_norm[None, :] if last else norm_in[l + 1][None, :]
        (xp, hp), st_p = _layer(p, l, xp, hp, mk_p, mv_p, l, states_p, 0, st_p, g_next, last)
        (xs, hs), st_s = _layer(p, l, xs, hs, cache_mem_k, cache_mem_v, l, states_s, l, st_s, g_next, last)

    def unpack(st, B):
        conv, ssd = st["a"]
        sre, sim = st["b"]
        c, n, m = st["c"]
        return (conv, ssd, sre.reshape(DEPTH, B, S5_GROUPS, S5_STATE), sim.reshape(DEPTH, B, S5_GROUPS, S5_STATE),
                c, n, m[:, :, 0, :ML_HEADS])

    return (xp, xs, mk_out, mv_out) + unpack(st_p, Bp) + unpack(st_s, Bs)
```

```python
import functools
import math

import jax
import jax.numpy as jnp
from jax import lax
from jax.experimental import pallas as pl
from jax.experimental.pallas import tpu as pltpu

F32 = jnp.float32
BF16 = jnp.bfloat16
NEG_INF = float("-inf")

D_MODEL = 1024
DEPTH = 4
BR_WIDTH = D_MODEL
SSD_HEADS = 16
SSD_HEAD_DIM = 64
SSD_GROUPS = 2
SSD_STATE = 128
SSD_CONV = 4
SSD_CONV_DIM = BR_WIDTH + 2 * SSD_GROUPS * SSD_STATE
S5_GROUP = 16
S5_GROUPS = 64
S5_STATE = 64
S5_CLUSTERS = 4
S5_CLUSTER_GROUPS = S5_GROUPS // S5_CLUSTERS
S5_FLAT = S5_GROUPS * S5_STATE
ML_HEADS = 4
ML_HEAD_DIM = 256
MEM_TOKENS = 256
XA_HEADS = 4
XA_HEAD_DIM = 256
EPS = 1e-6
LANES = 128
CONV_PAD = 8

IN_NAMES = ("z_ssd", "xbc", "dt", "u_s5", "z_s5", "q", "k", "v", "i", "f", "o", "z_ml", "q_xa", "z_xa", "gate")
IN_SIZES = (BR_WIDTH, SSD_CONV_DIM, SSD_HEADS, BR_WIDTH, BR_WIDTH, BR_WIDTH, BR_WIDTH, BR_WIDTH,
            ML_HEADS, ML_HEADS, BR_WIDTH, BR_WIDTH, BR_WIDTH, BR_WIDTH, 4 * D_MODEL)

WA_COLS = SSD_CONV_DIM + BR_WIDTH + LANES
WB_COLS = 2 * BR_WIDTH
WC_COLS = 3 * BR_WIDTH + 2 * LANES + 2 * BR_WIDTH
WD_COLS = 2 * BR_WIDTH

VMEM_LIMIT = 56 * 1024 * 1024


def _dot(a, b):
    return jnp.dot(a, b, preferred_element_type=F32)


def _dot_nt(a, b):
    return lax.dot_general(a, b, (((1,), (1,)), ((), ())), preferred_element_type=F32)


def _split3(x):
    hi = x.astype(BF16)
    r = x - hi.astype(F32)
    mid = r.astype(BF16)
    lo = (r - mid.astype(F32)).astype(BF16)
    return hi, mid, lo


def _sel_left(sel, x):
    hi, mid, lo = _split3(x)
    return _dot(sel, hi) + _dot(sel, mid) + _dot(sel, lo)


def _sel_right(x, sel):
    hi, mid, lo = _split3(x)
    return _dot(hi, sel) + _dot(mid, sel) + _dot(lo, sel)


def _softplus(x):
    return jnp.maximum(x, 0.0) + jnp.log1p(jnp.exp(-jnp.abs(x)))


def _silu(x):
    return x * jax.nn.sigmoid(x)


def _rms(x, g):
    return x * lax.rsqrt(jnp.mean(x * x, axis=-1, keepdims=True) + EPS) * g


def _seq_masks(nb, L):
    M = nb * L
    ri = lax.broadcasted_iota(jnp.int32, (M, M), 0)
    ci = lax.broadcasted_iota(jnp.int32, (M, M), 1)
    if nb > 1:
        sh = int(math.log2(L))
        same = lax.shift_right_logical(ri, sh) == lax.shift_right_logical(ci, sh)
        causal = same & (ci <= ri)
        upper = same & (ri <= ci)
    else:
        same = None
        causal = ci <= ri
        upper = ri <= ci
    return same, causal, upper


def _valid_masks(nb, L, valid):
    M = nb * L
    rv = (lax.broadcasted_iota(jnp.int32, (M, 1), 0) & (L - 1)) < valid
    cv = (lax.broadcasted_iota(jnp.int32, (1, M), 1) & (L - 1)) < valid
    return rv, cv


def _row_sel(nb, L, b, dtype):
    rows = lax.broadcasted_iota(jnp.int32, (nb * L, 1), 0)
    return (lax.shift_right_logical(rows, int(math.log2(L))) == b).astype(dtype)


def _col_pieces(n_cols, width):
    return [(c0, min(c0 + width, n_cols)) for c0 in range(0, n_cols, width)]


class _Spread:
    def __init__(self, thunks, n_slots):
        self.thunks, self.n_slots, self.calls, self.done = list(thunks), n_slots, 0, 0

    def slot(self):
        self.calls += 1
        while self.done < len(self.thunks) and self.done * self.n_slots < self.calls * len(self.thunks):
            self.thunks[self.done]()
            self.done += 1

    def flush(self):
        while self.done < len(self.thunks):
            self.thunks[self.done]()
            self.done += 1


def _proj_piece(proj_scr, slot, hb, w_ref, c0, c1):
    proj_scr[slot, :, c0:c1] = _dot(hb, w_ref[:, c0:c1])


def _next_chunk_spec(nb, L, nsub, n_steps, n_blocks):
    def index(i, c):
        nxt = jnp.minimum(i * n_steps + c + 1, n_blocks * n_steps - 1)
        return (nxt // n_steps, (nxt % n_steps) * nsub, 0)
    return pl.BlockSpec((nb, L, D_MODEL), index)


def _fill_padded_proj(proj_scr, p_ref, nb, L, valid):
    @pl.when((pl.program_id(0) == 0) & (pl.program_id(1) == 0))
    def _zero():
        proj_scr[0] = jnp.zeros(proj_scr.shape[1:], F32)

    for b in range(nb):
        proj_scr[0, b * L:b * L + valid, :] = p_ref[b]


def _proj_body(h_ref, w_ref, o_ref):
    o_ref[...] = _dot(h_ref[...], w_ref[...])


def _proj_call(h2d, w, l, tn):
    R = h2d.shape[0]
    cols = w.shape[-1]
    return pl.pallas_call(
        _proj_body,
        grid=(cols // tn,),
        in_specs=[pl.BlockSpec((R, D_MODEL), lambda j: (0, 0)),
                  pl.BlockSpec((None, D_MODEL, tn), lambda j: (l, 0, j))],
        out_specs=pl.BlockSpec((R, tn), lambda j: (0, j)),
        out_shape=jax.ShapeDtypeStruct((R, cols), F32),
        compiler_params=pltpu.CompilerParams(dimension_semantics=("arbitrary",), vmem_limit_bytes=VMEM_LIMIT),
        name="in_proj",
    )(h2d, w)


def _layer_spec(shape, l):
    return pl.BlockSpec((None,) + shape, lambda *_: (l,) + (0,) * len(shape), pipeline_mode=pl.Buffered(1))


def _state_spec(shape, l):
    return pl.BlockSpec((None,) + shape, lambda i, c: (l, i) + (0,) * (len(shape) - 1))


def _branch_call(body, *, name, grid, in_specs, args, y_spec, y_shape, st_specs, st_shapes, prev, scratch):
    n_in = len(args)
    prev = () if prev is None else tuple(prev)

    def wrapped(*refs):
        body(*refs[:n_in], *refs[n_in + len(prev):])

    outs = pl.pallas_call(
        wrapped,
        grid=grid,
        in_specs=list(in_specs) + [pl.BlockSpec(memory_space=pl.ANY)] * len(prev),
        out_specs=[y_spec] + list(st_specs),
        out_shape=[y_shape] + list(st_shapes),
        scratch_shapes=scratch,
        input_output_aliases={n_in + k: 1 + k for k in range(len(prev))},
        compiler_params=pltpu.CompilerParams(dimension_semantics=("arbitrary", "arbitrary"),
                                             vmem_limit_bytes=VMEM_LIMIT),
        name=name,
    )(*args, *prev)
    return outs[0], tuple(outs[1:])


def _norm_body(x_ref, g_ref, h_ref):
    h_ref[0] = _rms(x_ref[0], g_ref[...]).astype(BF16)


def _norm_call(x, g, tm):
    B, T, _ = x.shape
    return pl.pallas_call(
        _norm_body,
        grid=(B, T // tm),
        in_specs=[pl.BlockSpec((1, tm, D_MODEL), lambda b, t: (b, t, 0)),
                  pl.BlockSpec((1, D_MODEL), lambda b, t: (0, 0))],
        out_specs=pl.BlockSpec((1, tm, D_MODEL), lambda b, t: (b, t, 0)),
        out_shape=jax.ShapeDtypeStruct((B, T, D_MODEL), BF16),
        compiler_params=pltpu.CompilerParams(dimension_semantics=("arbitrary", "arbitrary")),
        name="rmsnorm_in",
    )(x, g)


def _memkv_body(mem_ref, g_ref, w_ref, mk_ref, mv_ref, mk5_ref, mv5_ref):
    hb = _rms(mem_ref[0], g_ref[0]).astype(BF16)
    kv = _dot(hb, w_ref[0])
    mk_ref[0, 0] = kv[:, :BR_WIDTH]
    mv_ref[0, 0] = kv[:, BR_WIDTH:]
    for hd in range(XA_HEADS):
        lo, hi = hd * XA_HEAD_DIM, (hd + 1) * XA_HEAD_DIM
        mk5_ref[0, 0, :, hd, :] = kv[:, lo:hi]
        mv5_ref[0, 0, :, hd, :] = kv[:, BR_WIDTH + lo:BR_WIDTH + hi]


def _memkv_call(mem, g, w_kv):
    B = mem.shape[0]
    out = jax.ShapeDtypeStruct((DEPTH, B, MEM_TOKENS, BR_WIDTH), F32)
    out5 = jax.ShapeDtypeStruct((DEPTH, B, MEM_TOKENS, XA_HEADS, XA_HEAD_DIM), F32)
    spec5 = pl.BlockSpec((1, 1, MEM_TOKENS, XA_HEADS, XA_HEAD_DIM), lambda l, b: (l, b, 0, 0, 0))
    return pl.pallas_call(
        _memkv_body,
        grid=(DEPTH, B),
        in_specs=[pl.BlockSpec((1, MEM_TOKENS, D_MODEL), lambda l, b: (b, 0, 0)),
                  pl.BlockSpec((1, 1, D_MODEL), lambda l, b: (l, 0, 0)),
                  pl.BlockSpec((1, D_MODEL, 2 * BR_WIDTH), lambda l, b: (l, 0, 0))],
        out_specs=[pl.BlockSpec((1, 1, MEM_TOKENS, BR_WIDTH), lambda l, b: (l, b, 0, 0)),
                   pl.BlockSpec((1, 1, MEM_TOKENS, BR_WIDTH), lambda l, b: (l, b, 0, 0)), spec5, spec5],
        out_shape=[out, out, out5, out5],
        compiler_params=pltpu.CompilerParams(dimension_semantics=("arbitrary", "arbitrary"),
                                             vmem_limit_bytes=VMEM_LIMIT),
        name="mem_kv",
    )(mem, g, w_kv)


def _ssd_body(h_ref, hn_ref, w_ref, wst_ref, prow_ref, pcol_ref, cw_ref, cb_ref, dsk_ref, nrm_ref, buf0_ref,
              s0_ref, y_ref, bufo_ref, so_ref, xp_scr, yin_scr, yst_scr, proj_scr, *, nb, L, valid, nsub):
    M = nb * L
    pipelined = nsub % 2 == 0
    chunk = lambda ref, s: ref[:, s * L:(s + 1) * L, :].reshape(M, D_MODEL)

    @pl.when(pl.program_id(1) == 0)
    def _init():
        so_ref[...] = s0_ref[...]
        xp_scr[:, CONV_PAD - 3:CONV_PAD, :] = buf0_ref[...]

    if pipelined:
        @pl.when((pl.program_id(0) == 0) & (pl.program_id(1) == 0))
        def _prologue():
            proj_scr[0] = _dot(chunk(h_ref, 0), w_ref[...])
    else:
        assert nsub == 1
        _fill_padded_proj(proj_scr, w_ref, nb, L, valid)

    for s in range(nsub):
        hb_next = chunk(h_ref, s + 1) if s + 1 < nsub else (hn_ref[...].reshape(M, D_MODEL) if pipelined else None)
        thunks = [] if hb_next is None else [
            functools.partial(_proj_piece, proj_scr, (s + 1) % 2, hb_next, w_ref, c0, c1)
            for c0, c1 in _col_pieces(WA_COLS, 512)]
        _ssd_chunk(chunk(h_ref, s), proj_scr.at[s % 2], s, _Spread(thunks, SSD_HEADS * (1 + nb)),
                   wst_ref, prow_ref, pcol_ref, cw_ref, cb_ref, dsk_ref, nrm_ref,
                   y_ref, bufo_ref, so_ref, xp_scr, yin_scr, yst_scr, nb=nb, L=L, valid=valid)


def _ssd_chunk(hb, proj, s, spread, wst_ref, prow_ref, pcol_ref, cw_ref, cb_ref, dsk_ref, nrm_ref,
               y_ref, bufo_ref, so_ref, xp_scr, yin_scr, yst_scr, *, nb, L, valid):
    M = nb * L
    hg = SSD_HEADS // SSD_GROUPS

    xp_scr[:, CONV_PAD:CONV_PAD + L, :] = proj[:, :SSD_CONV_DIM].reshape(nb, L, SSD_CONV_DIM)
    conv = cb_ref[...][None]
    for k in range(SSD_CONV):
        o = CONV_PAD - 3 + k
        conv = conv + xp_scr[:, o:o + L, :] * cw_ref[k:k + 1, :][None]
    nbuf = xp_scr[:, CONV_PAD - 3 + valid:CONV_PAD + valid, :]
    xp_scr[:, CONV_PAD - 3:CONV_PAD, :] = nbuf
    bufo_ref[...] = nbuf
    xbc = _silu(conv).reshape(M, SSD_CONV_DIM)
    xs = xbc[:, :BR_WIDTH]
    bm = xbc[:, BR_WIDTH:BR_WIDTH + SSD_GROUPS * SSD_STATE].astype(BF16)
    cm = xbc[:, BR_WIDTH + SSD_GROUPS * SSD_STATE:].astype(BF16)
    z = proj[:, SSD_CONV_DIM:SSD_CONV_DIM + BR_WIDTH]

    same, causal, upper = _seq_masks(nb, L)
    tril_b = causal.astype(BF16)
    triu_b = upper.astype(BF16)
    same_b = jnp.ones((M, M), BF16) if same is None else same.astype(BF16)

    dt = _softplus(proj[:, SSD_CONV_DIM + BR_WIDTH:] + prow_ref[0:1, :])
    dtT = _softplus(_dot_nt(wst_ref[...], hb) + pcol_ref[:, 0:1])
    dt = jnp.where(lax.broadcasted_iota(jnp.int32, (1, LANES), 1) < SSD_HEADS, dt, 0.0)
    if valid < L:
        rv, cv = _valid_masks(nb, L, valid)
        dt = jnp.where(rv, dt, 0.0)
        dtT = jnp.where(cv, dtT, 0.0)
    la = dt * (-jnp.exp(prow_ref[1:2, :]))
    laT = dtT * (-jnp.exp(pcol_ref[:, 1:2]))
    acs = _sel_left(tril_b, la)
    tot = _sel_left(same_b, la)
    acsT = _sel_right(laT, triu_b)

    totT = _sel_right(laT, same_b)
    tailT = jnp.exp(totT - acsT)
    eend = jnp.exp(tot)
    expand = (lax.shift_right_logical(lax.broadcasted_iota(jnp.int32, (LANES, BR_WIDTH), 1), 6)
              == lax.broadcasted_iota(jnp.int32, (LANES, BR_WIDTH), 0)).astype(BF16)
    eacs_x = _sel_right(jnp.exp(acs), expand)

    xsT = xs.T
    wT = dtT * tailT

    for g in range(SSD_GROUPS):
        cm_g = cm[:, g * SSD_STATE:(g + 1) * SSD_STATE]
        bm_g = bm[:, g * SSD_STATE:(g + 1) * SSD_STATE]
        cb = _dot_nt(cm_g, bm_g)
        ats = []
        for hh in range(hg):
            h = g * hg + hh
            lo, hi = h * SSD_HEAD_DIM, (h + 1) * SSD_HEAD_DIM
            seg = acs[:, h:h + 1] - acsT[h:h + 1, :]
            decay = jnp.exp(jnp.where(causal, seg, NEG_INF))
            xdtT = (xsT[lo:hi, :] * dtT[h:h + 1, :]).astype(BF16)
            ats.append((xsT[lo:hi, :] * wT[h:h + 1, :]).astype(BF16))
            yin_scr[:, lo:hi] = _dot_nt((cb * decay).astype(BF16), xdtT)
            spread.slot()
        at_g = jnp.concatenate(ats, axis=0)
        glo, ghi = g * hg, (g + 1) * hg
        for b in range(nb):
            r0, r1 = b * L, (b + 1) * L
            bm_b = bm_g if nb == 1 else bm_g * _row_sel(nb, L, b, BF16)
            s_old = so_ref[b, glo:ghi].reshape(hg * SSD_HEAD_DIM, SSD_STATE)
            yst_scr[r0:r1, glo * SSD_HEAD_DIM:ghi * SSD_HEAD_DIM] = _dot_nt(cm_g[r0:r1, :], s_old.astype(BF16))
            kept = jnp.concatenate(
                [eend[r0:r0 + 1, glo + hh:glo + hh + 1] * s_old[hh * SSD_HEAD_DIM:(hh + 1) * SSD_HEAD_DIM]
                 for hh in range(hg)], axis=0)
            so_ref[b, glo:ghi] = (kept + _dot(at_g, bm_b)).reshape(hg, SSD_HEAD_DIM, SSD_STATE)
            for _ in range(hg):
                spread.slot()
    spread.flush()

    y = yin_scr[...] + eacs_x * yst_scr[...] + dsk_ref[...] * xs
    y_ref[:, s * L:(s + 1) * L, :] = _rms(y * _silu(z), nrm_ref[...]).reshape(nb, L, BR_WIDTH).astype(BF16)


def _ssd_call(h, p, buf0, s0, l, ls, prev, *, nb, L, valid, nsub, proj=None):
    B, T, _ = h.shape
    wspec = lambda shape: _layer_spec(shape, l)
    conv_blk = (nb, SSD_CONV - 1, SSD_CONV_DIM)
    ssd_blk = (nb, SSD_HEADS, SSD_HEAD_DIM, SSD_STATE)
    w_arg, w_spec = ((p["wA"], wspec((D_MODEL, WA_COLS))) if proj is None else
                     (proj, pl.BlockSpec((nb, valid, WA_COLS), lambda i, c: (i, 0, 0))))
    return _branch_call(
        functools.partial(_ssd_body, nb=nb, L=L, valid=valid, nsub=nsub),
        name="ssd_branch",
        grid=(B // nb, T // (nsub * L)),
        in_specs=[pl.BlockSpec((nb, nsub * L, D_MODEL), lambda i, c: (i, c, 0)),
                  _next_chunk_spec(nb, L, nsub, T // (nsub * L), B // nb),
                  w_spec, wspec((SSD_HEADS, D_MODEL)),
                  wspec((8, LANES)), wspec((SSD_HEADS, 2)),
                  wspec((SSD_CONV, SSD_CONV_DIM)), wspec((1, SSD_CONV_DIM)),
                  wspec((1, BR_WIDTH)), wspec((1, BR_WIDTH)),
                  _state_spec(conv_blk, ls), _state_spec(ssd_blk, ls)],
        args=(h, h, w_arg, p["wstA"], p["prowA"], p["pcolA"], p["conv_w"], p["conv_b"], p["dskA"], p["nrmA"],
              buf0, s0),
        y_spec=pl.BlockSpec((nb, nsub * L, BR_WIDTH), lambda i, c: (i, c, 0)),
        y_shape=jax.ShapeDtypeStruct((B, T, BR_WIDTH), BF16),
        st_specs=[_state_spec(conv_blk, l), _state_spec(ssd_blk, l)],
        st_shapes=[jax.ShapeDtypeStruct((DEPTH, B) + conv_blk[1:], F32),
                   jax.ShapeDtypeStruct((DEPTH, B) + ssd_blk[1:], F32)],
        prev=prev,
        scratch=[pltpu.VMEM((nb, L + CONV_PAD, SSD_CONV_DIM), F32),
                 pltpu.VMEM((nb * L, BR_WIDTH), F32),
                 pltpu.VMEM((nb * L, BR_WIDTH), F32),
                 pltpu.VMEM((2, nb * L, WA_COLS), F32)])


S5_SCAN_LANES = 512


def _s5_body(h_ref, w_ref, bc_ref, cc_ref, lam_ref, dsk_ref, gw_ref, gb_ref, sre0_ref, sim0_ref,
             y_ref, sre_ref, sim_ref, hs_scr, st_scr, *, nb, Lt):
    half = S5_CLUSTER_GROUPS * S5_STATE

    @pl.when(pl.program_id(1) == 0)
    def _load_state():
        for g in range(S5_GROUPS):
            st_scr[0, :, g * S5_STATE:(g + 1) * S5_STATE] = sre0_ref[pl.ds(g, nb, stride=S5_GROUPS), :]
            st_scr[1, :, g * S5_STATE:(g + 1) * S5_STATE] = sim0_ref[pl.ds(g, nb, stride=S5_GROUPS), :]

    hb = jnp.swapaxes(h_ref[...].astype(F32), 0, 1).reshape(nb * Lt, D_MODEL).astype(BF16)
    proj = _dot(hb, w_ref[...])
    u = proj[:, :BR_WIDTH]
    z = proj[:, BR_WIDTH:]
    ub = u.astype(BF16)
    gl = S5_CLUSTER_GROUPS * S5_GROUP
    def b_u(j):
        hs_scr[:, 2 * half * j:2 * half * (j + 1)] = _dot(ub[:, gl * j:gl * (j + 1)], bc_ref[j])

    ys = []
    b_u(0)
    for j in range(S5_CLUSTERS):
        if j + 1 < S5_CLUSTERS:
            b_u(j + 1)
        for q in range(half // S5_SCAN_LANES):
            cr = 2 * half * j + S5_SCAN_LANES * q
            ci = cr + half
            sc = half * j + S5_SCAN_LANES * q
            lr = jnp.broadcast_to(lam_ref[0:1, sc:sc + S5_SCAN_LANES], (8, S5_SCAN_LANES))
            li = jnp.broadcast_to(lam_ref[1:2, sc:sc + S5_SCAN_LANES], (8, S5_SCAN_LANES))
            for sg in range(nb // 8):
                hr = st_scr[0, 8 * sg:8 * sg + 8, sc:sc + S5_SCAN_LANES]
                hi = st_scr[1, 8 * sg:8 * sg + 8, sc:sc + S5_SCAN_LANES]
                for t in range(Lt):
                    r0 = t * nb + 8 * sg
                    nr = lr * hr - li * hi + hs_scr[r0:r0 + 8, cr:cr + S5_SCAN_LANES]
                    ni = lr * hi + li * hr + hs_scr[r0:r0 + 8, ci:ci + S5_SCAN_LANES]
                    hs_scr[r0:r0 + 8, cr:cr + S5_SCAN_LANES] = nr
                    hs_scr[r0:r0 + 8, ci:ci + S5_SCAN_LANES] = ni
                    hr, hi = nr, ni
                st_scr[0, 8 * sg:8 * sg + 8, sc:sc + S5_SCAN_LANES] = hr
                st_scr[1, 8 * sg:8 * sg + 8, sc:sc + S5_SCAN_LANES] = hi
        ys.append(_dot(hs_scr[:, 2 * half * j:2 * half * (j + 1)].astype(BF16), cc_ref[j]))

    y = jnp.concatenate(ys, axis=1) + dsk_ref[...] * u
    yb = jax.nn.gelu(y)
    glu = jax.nn.sigmoid(_dot(yb.astype(BF16), gw_ref[...]) + gb_ref[...])
    out = (yb * glu * _silu(z)).reshape(Lt, nb, BR_WIDTH)
    y_ref[...] = jnp.swapaxes(out, 0, 1).astype(BF16)

    @pl.when(pl.program_id(1) == pl.num_programs(1) - 1)
    def _store_state():
        for g in range(S5_GROUPS):
            sre_ref[pl.ds(g, nb, stride=S5_GROUPS), :] = st_scr[0, :, g * S5_STATE:(g + 1) * S5_STATE]
            sim_ref[pl.ds(g, nb, stride=S5_GROUPS), :] = st_scr[1, :, g * S5_STATE:(g + 1) * S5_STATE]


def _s5_call(h, p, sre0, sim0, l, ls, prev, *, nb, Lt):
    B, T, _ = h.shape
    wspec = lambda shape: _layer_spec(shape, l)
    st_blk = (nb * S5_GROUPS, S5_STATE)
    st_shape = jax.ShapeDtypeStruct((DEPTH, B * S5_GROUPS, S5_STATE), F32)
    return _branch_call(
        functools.partial(_s5_body, nb=nb, Lt=Lt),
        name="s5_branch",
        grid=(B // nb, T // Lt),
        in_specs=[pl.BlockSpec((nb, Lt, D_MODEL), lambda i, c: (i, c, 0)),
                  wspec((D_MODEL, WB_COLS)),
                  wspec((S5_CLUSTERS, S5_CLUSTER_GROUPS * S5_GROUP, 2 * S5_CLUSTER_GROUPS * S5_STATE)),
                  wspec((S5_CLUSTERS, 2 * S5_CLUSTER_GROUPS * S5_STATE, S5_CLUSTER_GROUPS * S5_GROUP)),
                  wspec((2, S5_FLAT)), wspec((1, BR_WIDTH)),
                  wspec((BR_WIDTH, BR_WIDTH)), wspec((1, BR_WIDTH)),
                  _state_spec(st_blk, ls), _state_spec(st_blk, ls)],
        args=(h, p["wB"], p["bc"], p["cc"], p["lam"], p["dskB"], p["glu_w"], p["glu_b"], sre0, sim0),
        y_spec=pl.BlockSpec((nb, Lt, BR_WIDTH), lambda i, c: (i, c, 0)),
        y_shape=jax.ShapeDtypeStruct((B, T, BR_WIDTH), BF16),
        st_specs=[_state_spec(st_blk, l), _state_spec(st_blk, l)],
        st_shapes=[st_shape, st_shape],
        prev=prev,
        scratch=[pltpu.VMEM((nb * Lt, 2 * S5_FLAT), F32), pltpu.VMEM((2, nb, S5_FLAT), F32)])


def _mlstm_body(h_ref, hn_ref, w_ref, wst_ref, prow_ref, pcol_ref, nrm_ref, c0_ref, n0_ref, m0_ref,
                y_ref, co_ref, no_ref, mo_ref, num_scr, proj_scr, *, nb, L, valid, nsub):
    M = nb * L
    pipelined = nsub % 2 == 0
    chunk = lambda ref, s: ref[:, s * L:(s + 1) * L, :].reshape(M, D_MODEL)

    @pl.when(pl.program_id(1) == 0)
    def _init():
        co_ref[...] = c0_ref[...]
        no_ref[...] = n0_ref[...]
        mo_ref[...] = m0_ref[...]

    if pipelined:
        @pl.when((pl.program_id(0) == 0) & (pl.program_id(1) == 0))
        def _prologue():
            proj_scr[0] = _dot(chunk(h_ref, 0), w_ref[...])
    else:
        assert nsub == 1
        proj_scr[0] = _dot(chunk(h_ref, 0), w_ref[...])

    for s in range(nsub):
        hb_next = chunk(h_ref, s + 1) if s + 1 < nsub else (hn_ref[...].reshape(M, D_MODEL) if pipelined else None)
        thunks = [] if hb_next is None else [
            functools.partial(_proj_piece, proj_scr, (s + 1) % 2, hb_next, w_ref, c0, c1)
            for c0, c1 in _col_pieces(WC_COLS, 768)]
        _mlstm_chunk(chunk(h_ref, s), proj_scr.at[s % 2], s, _Spread(thunks, ML_HEADS * (1 + nb)),
                     wst_ref, prow_ref, pcol_ref, nrm_ref, y_ref, co_ref, no_ref, mo_ref, num_scr,
                     nb=nb, L=L, valid=valid)


def _mlstm_chunk(hb, proj, s, spread, wst_ref, prow_ref, pcol_ref, nrm_ref, y_ref, co_ref, no_ref, mo_ref,
                 num_scr, *, nb, L, valid):
    M = nb * L
    W = BR_WIDTH
    q = proj[:, :W]
    k = proj[:, W:2 * W] * (ML_HEAD_DIM ** -0.5)
    v = proj[:, 2 * W:3 * W]
    ig = proj[:, 3 * W:3 * W + LANES] + prow_ref[0:1, :]
    logf = -_softplus(-(proj[:, 3 * W + LANES:3 * W + 2 * LANES] + prow_ref[1:2, :]))
    o = proj[:, 3 * W + 2 * LANES:4 * W + 2 * LANES]
    z = proj[:, 4 * W + 2 * LANES:]
    smallT = _dot_nt(wst_ref[...], hb) + pcol_ref[...]
    igT = smallT[0:8, :]
    logfT = -_softplus(-smallT[8:16, :])
    if valid < L:
        rv, cv = _valid_masks(nb, L, valid)
        ig = jnp.where(rv, ig, NEG_INF)
        logf = jnp.where(rv, logf, 0.0)
        igT = jnp.where(cv, igT, NEG_INF)
        logfT = jnp.where(cv, logfT, 0.0)

    same, causal, upper = _seq_masks(nb, L)
    tril_b = causal.astype(BF16)
    triu_b = upper.astype(BF16)
    same_b = jnp.ones((M, M), BF16) if same is None else same.astype(BF16)
    bcum = _sel_left(tril_b, logf)
    tot = _sel_left(same_b, logf)
    bcumT = _sel_right(logfT, triu_b)
    totT = _sel_right(logfT, same_b)
    m_rows = jnp.broadcast_to(mo_ref[...], (nb, L, LANES)).reshape(M, LANES)
    gcar = bcum + m_rows
    g_end = tot + m_rows
    d_end = tot - bcum + ig
    d_endT = totT - bcumT + igT

    qb = q.astype(BF16)
    kb = k.astype(BF16)
    vb = v.astype(BF16)
    lane = lax.broadcasted_iota(jnp.int32, (M, LANES), 1)
    m_new_all = jnp.zeros((M, LANES), F32)
    per_head = []
    wk_parts = []
    for hd in range(ML_HEADS):
        lo, hi = hd * ML_HEAD_DIM, (hd + 1) * ML_HEAD_DIM
        dmat = jnp.where(causal, bcum[:, hd:hd + 1] - bcumT[hd:hd + 1, :] + igT[hd:hd + 1, :], NEG_INF)
        m_l = jnp.maximum(gcar[:, hd:hd + 1], jnp.max(dmat, axis=1, keepdims=True))
        w_inter = jnp.exp(gcar[:, hd:hd + 1] - m_l)
        qk = _dot_nt(qb[:, lo:hi], kb[:, lo:hi]) * jnp.exp(dmat - m_l)
        num_scr[:, lo:hi] = _dot(qk.astype(BF16), vb[:, lo:hi])
        den_intra = jnp.sum(qk, axis=1, keepdims=True)
        d_row = d_endT[hd:hd + 1, :] if same is None else jnp.where(same, d_endT[hd:hd + 1, :], NEG_INF)
        m_new = jnp.maximum(g_end[:, hd:hd + 1], jnp.max(d_row, axis=1, keepdims=True))
        w_end = jnp.exp(d_end[:, hd:hd + 1] - m_new)
        dec = jnp.exp(g_end[:, hd:hd + 1] - m_new)
        wk_parts.append(w_end * k[:, lo:hi])
        m_new_all = jnp.where(lane == hd, m_new, m_new_all)
        per_head.append((m_l, w_inter, den_intra, dec))
        spread.slot()
    wk = jnp.concatenate(wk_parts, axis=1)
    kt = wk.T.astype(BF16)

    for b in range(nb):
        r0, r1 = b * L, (b + 1) * L
        sel = None if nb == 1 else _row_sel(nb, L, b, BF16)
        for hd in range(ML_HEADS):
            lo, hi = hd * ML_HEAD_DIM, (hd + 1) * ML_HEAD_DIM
            m_l, w_inter, den_intra, dec = per_head[hd]
            c_old = co_ref[b, hd]
            n_old = no_ref[b, hd:hd + 1, :]
            q_r = q[r0:r1, lo:hi]
            wi = w_inter[r0:r1]
            num = num_scr[r0:r1, lo:hi] + wi * _dot(qb[r0:r1, lo:hi], c_old.astype(BF16))
            den = den_intra[r0:r1] + wi * jnp.sum(q_r * n_old, axis=1, keepdims=True)
            hout = num / jnp.maximum(jnp.abs(den), jnp.exp(-m_l[r0:r1]))
            ho = jax.nn.sigmoid(o[r0:r1, lo:hi]) * hout
            yv = _rms(ho, nrm_ref[:, lo:hi]) * _silu(z[r0:r1, lo:hi])
            y_ref[b, s * L:(s + 1) * L, lo:hi] = yv.astype(BF16)
            dec_b = dec[r0:r0 + 1]
            vb_b = vb[:, lo:hi] if sel is None else vb[:, lo:hi] * sel
            co_ref[b, hd] = dec_b * c_old + _dot(kt[lo:hi, :], vb_b)
            no_ref[b, hd:hd + 1, :] = dec_b * n_old + jnp.sum(wk[r0:r1, lo:hi], axis=0, keepdims=True)
            spread.slot()
        mo_ref[b] = m_new_all[r0:r0 + 1, :]
    spread.flush()


def _mlstm_call(h, p, c0, n0, m0, l, ls, prev, *, nb, L, valid, nsub):
    B, T, _ = h.shape
    wspec = lambda shape: _layer_spec(shape, l)
    w_arg, w_spec = p["wC"], wspec((D_MODEL, WC_COLS))
    c_blk = (nb, ML_HEADS, ML_HEAD_DIM, ML_HEAD_DIM)
    n_blk = (nb, ML_HEADS, ML_HEAD_DIM)
    m_blk = (nb, 1, LANES)
    return _branch_call(
        functools.partial(_mlstm_body, nb=nb, L=L, valid=valid, nsub=nsub),
        name="mlstm_branch",
        grid=(B // nb, T // (nsub * L)),
        in_specs=[pl.BlockSpec((nb, nsub * L, D_MODEL), lambda i, c: (i, c, 0)),
                  _next_chunk_spec(nb, L, nsub, T // (nsub * L), B // nb),
                  w_spec, wspec((16, D_MODEL)),
                  wspec((8, LANES)), wspec((16, 1)), wspec((1, BR_WIDTH)),
                  _state_spec(c_blk, ls), _state_spec(n_blk, ls), _state_spec(m_blk, ls)],
        args=(h, h, w_arg, p["wstC"], p["prowC"], p["pcolC"], p["nrmC"], c0, n0, m0),
        y_spec=pl.BlockSpec((nb, nsub * L, BR_WIDTH), lambda i, c: (i, c, 0)),
        y_shape=jax.ShapeDtypeStruct((B, T, BR_WIDTH), BF16),
        st_specs=[_state_spec(c_blk, l), _state_spec(n_blk, l), _state_spec(m_blk, l)],
        st_shapes=[jax.ShapeDtypeStruct((DEPTH, B) + c_blk[1:], F32),
                   jax.ShapeDtypeStruct((DEPTH, B) + n_blk[1:], F32),
                   jax.ShapeDtypeStruct((DEPTH, B) + m_blk[1:], F32)],
        prev=prev,
        scratch=[pltpu.VMEM((nb * L, BR_WIDTH), F32), pltpu.VMEM((2, nb * L, WC_COLS), F32)])


def _xattn_body(h_ref, hn_ref, w_ref, mk_ref, mv_ref, y_ref, proj_scr, *, L, nsub):
    @pl.when((pl.program_id(0) == 0) & (pl.program_id(1) == 0))
    def _prologue():
        proj_scr[0] = _dot(h_ref[0, 0:L, :], w_ref[...])

    for s in range(nsub):
        hb_next = h_ref[0, (s + 1) * L:(s + 2) * L, :] if s + 1 < nsub else hn_ref[0]
        spread = _Spread([functools.partial(_proj_piece, proj_scr, (s + 1) % 2, hb_next, w_ref, c0, c1)
                          for c0, c1 in _col_pieces(WD_COLS, 512)], XA_HEADS)
        proj = proj_scr.at[s % 2]
        for hd in range(XA_HEADS):
            lo, hi = hd * XA_HEAD_DIM, (hd + 1) * XA_HEAD_DIM
            sc = _dot_nt(proj[:, lo:hi].astype(BF16), mk_ref[0, :, lo:hi].astype(BF16)) * (XA_HEAD_DIM ** -0.5)
            e = jnp.exp(sc - jnp.max(sc, axis=1, keepdims=True))
            p = e / jnp.sum(e, axis=1, keepdims=True)
            a = _dot(p.astype(BF16), mv_ref[0, :, lo:hi].astype(BF16))
            y_ref[0, s * L:(s + 1) * L, lo:hi] = (a * _silu(proj[:, BR_WIDTH + lo:BR_WIDTH + hi])).astype(BF16)
            spread.slot()
        spread.flush()


def _xattn_rows_body(p_ref, mk_ref, mv_ref, y_ref, pj_scr, *, nb, L, valid):
    R = MEM_TOKENS * XA_HEADS
    Q = XA_HEADS * L

    @pl.when(pl.program_id(0) == 0)
    def _zero():
        pj_scr[...] = jnp.zeros(pj_scr.shape, F32)

    for b in range(nb):
        pj_scr[b * L:b * L + valid, :] = p_ref[b]
    q = pj_scr[:, :BR_WIDTH]
    z = pj_scr[:, BR_WIDTH:]
    row_head = lax.shift_right_logical(lax.broadcasted_iota(jnp.int32, (Q, R), 0), int(math.log2(L)))
    col_head = lax.broadcasted_iota(jnp.int32, (Q, R), 1) & (XA_HEADS - 1)
    own = row_head == col_head
    for b in range(nb):
        r0, r1 = b * L, (b + 1) * L
        qall = jnp.concatenate([q[r0:r1, hd * XA_HEAD_DIM:(hd + 1) * XA_HEAD_DIM] for hd in range(XA_HEADS)],
                               axis=0).astype(BF16)
        mk2 = mk_ref[b].reshape(R, XA_HEAD_DIM).astype(BF16)
        mv2 = mv_ref[b].reshape(R, XA_HEAD_DIM).astype(BF16)
        s = jnp.where(own, _dot_nt(qall, mk2) * (XA_HEAD_DIM ** -0.5), -1e30)
        e = jnp.exp(s - jnp.max(s, axis=1, keepdims=True))
        pr = e / jnp.sum(e, axis=1, keepdims=True)
        a = _dot(pr.astype(BF16), mv2)
        for hd in range(XA_HEADS):
            lo, hi = hd * XA_HEAD_DIM, (hd + 1) * XA_HEAD_DIM
            yv = a[hd * L:(hd + 1) * L, :] * _silu(z[r0:r1, lo:hi])
            y_ref[b, :, lo:hi] = yv[:valid].astype(BF16)


def _xattn_rows_call(proj, mk, mv, lkv, *, nb, L):
    B, T, _ = proj.shape
    assert T <= L
    kv_spec = pl.BlockSpec((None, nb, MEM_TOKENS, XA_HEADS, XA_HEAD_DIM), lambda i: (lkv, i, 0, 0, 0))
    return pl.pallas_call(
        functools.partial(_xattn_rows_body, nb=nb, L=L, valid=T),
        grid=(B // nb,),
        in_specs=[pl.BlockSpec((nb, T, WD_COLS), lambda i: (i, 0, 0)), kv_spec, kv_spec],
        out_specs=pl.BlockSpec((nb, T, BR_WIDTH), lambda i: (i, 0, 0)),
        out_shape=jax.ShapeDtypeStruct((B, T, BR_WIDTH), BF16),
        scratch_shapes=[pltpu.VMEM((nb * L, WD_COLS), F32)],
        compiler_params=pltpu.CompilerParams(dimension_semantics=("arbitrary",), vmem_limit_bytes=VMEM_LIMIT),
        name="xattn_rows",
    )(proj, mk, mv)


def _xattn_call(h, wD, mk, mv, l, lkv, *, L, nsub):
    B, T, _ = h.shape
    assert nsub % 2 == 0
    n_steps = T // (nsub * L)
    kv_spec = pl.BlockSpec((None, 1, MEM_TOKENS, BR_WIDTH), lambda i, c: (lkv, i, 0, 0))
    return pl.pallas_call(
        functools.partial(_xattn_body, L=L, nsub=nsub),
        grid=(B, n_steps),
        in_specs=[pl.BlockSpec((1, nsub * L, D_MODEL), lambda i, c: (i, c, 0)),
                  _next_chunk_spec(1, L, nsub, n_steps, B),
                  _layer_spec((D_MODEL, WD_COLS), l),
                  kv_spec, kv_spec],
        out_specs=pl.BlockSpec((1, nsub * L, BR_WIDTH), lambda i, c: (i, c, 0)),
        out_shape=jax.ShapeDtypeStruct((B, T, BR_WIDTH), BF16),
        scratch_shapes=[pltpu.VMEM((2, L, WD_COLS), F32)],
        compiler_params=pltpu.CompilerParams(dimension_semantics=("arbitrary", "arbitrary"),
                                             vmem_limit_bytes=VMEM_LIMIT),
        name="xattn_branch",
    )(h, h, wD, mk, mv)


def _merge_body(h_ref, ya_ref, yb_ref, yc_ref, yd_ref, x_ref, wg_ref, bg_ref, wd_ref, wo_ref, gn_ref, *outs, last):
    h = h_ref[0]
    ys = (ya_ref[0], yb_ref[0], yc_ref[0], yd_ref[0])
    merged = None
    for kbr in range(4):
        lo, hi = kbr * D_MODEL, (kbr + 1) * D_MODEL
        gate = jax.nn.sigmoid(_dot(h, wg_ref[:, lo:hi]) + bg_ref[:, lo:hi])
        term = gate * _dot(ys[kbr], wd_ref[kbr])
        merged = term if merged is None else merged + term
    xn = x_ref[0] + _dot(merged.astype(BF16), wo_ref[...])
    if last:
        outs[0][0] = _rms(xn, gn_ref[...])
    else:
        outs[0][0] = xn
        outs[1][0] = _rms(xn, gn_ref[...]).astype(BF16)


def _merge_call(h, ya, yb, yc, yd, x, wE, b_gate, w_down, w_out, g_next, l, *, tm, last):
    B, T, _ = x.shape
    row = pl.BlockSpec((1, tm, D_MODEL), lambda b, t: (b, t, 0))
    wspec = lambda shape: _layer_spec(shape, l)
    if last:
        out_specs = [row]
        out_shape = [jax.ShapeDtypeStruct((B, T, D_MODEL), F32)]
    else:
        out_specs = [row, row]
        out_shape = [jax.ShapeDtypeStruct((B, T, D_MODEL), F32),
                     jax.ShapeDtypeStruct((B, T, D_MODEL), BF16)]
    return pl.pallas_call(
        functools.partial(_merge_body, last=last),
        grid=(B, T // tm),
        in_specs=[row, row, row, row, row, row,
                  wspec((D_MODEL, 4 * D_MODEL)), wspec((1, 4 * D_MODEL)),
                  wspec((4, BR_WIDTH, D_MODEL)), wspec((D_MODEL, D_MODEL)),
                  pl.BlockSpec((1, D_MODEL), lambda b, t: (0, 0))],
        out_specs=out_specs,
        out_shape=out_shape,
        compiler_params=pltpu.CompilerParams(dimension_semantics=("arbitrary", "arbitrary"),
                                             vmem_limit_bytes=VMEM_LIMIT),
        name="merge_out",
    )(h, ya, yb, yc, yd, x, wE, b_gate, w_down, w_out, g_next)


def _pack_params(w_in, b_gate, b_igate, b_fgate, ssd_conv_w, ssd_conv_b, ssd_dt_bias, ssd_a_log, ssd_d, ssd_norm,
                 s5_a_re, s5_a_im, s5_log_dt, s5_b_re, s5_b_im, s5_c_re, s5_c_im, s5_d, s5_glu_w, s5_glu_b,
                 ml_norm, w_down, w_out):
    offs = {}
    acc = 0
    for name, size in zip(IN_NAMES, IN_SIZES):
        offs[name] = (acc, acc + size)
        acc += size
    col = lambda name: w_in[:, :, offs[name][0]:offs[name][1]]
    padl = lambda a: jnp.pad(a, ((0, 0), (0, 0), (0, LANES - a.shape[-1])))
    p = {}
    p["wA"] = jnp.concatenate([col("xbc"), col("z_ssd"), padl(col("dt"))], axis=-1).astype(BF16)
    p["wstA"] = jnp.swapaxes(col("dt"), 1, 2).astype(BF16)
    p["wB"] = jnp.concatenate([col("u_s5"), col("z_s5")], axis=-1).astype(BF16)
    p["wC"] = jnp.concatenate([col("q"), col("k"), col("v"), padl(col("i")), padl(col("f")), col("o"),
                               col("z_ml")], axis=-1).astype(BF16)
    zrow = jnp.zeros((DEPTH, 4, D_MODEL), F32)
    p["wstC"] = jnp.concatenate([jnp.swapaxes(col("i"), 1, 2), zrow, jnp.swapaxes(col("f"), 1, 2), zrow],
                                axis=1).astype(BF16)
    p["wD"] = jnp.concatenate([col("q_xa"), col("z_xa")], axis=-1).astype(BF16)
    p["wE"] = col("gate").astype(BF16)
    p["b_gate"] = b_gate[:, None, :]
    p["w_down"] = w_down.astype(BF16)
    p["w_out"] = w_out.astype(BF16)

    padv = lambda a: jnp.pad(a, ((0, 0), (0, LANES - a.shape[-1])))
    zl = jnp.zeros((DEPTH, 6, LANES), F32)
    p["prowA"] = jnp.concatenate([padv(ssd_dt_bias)[:, None], padv(ssd_a_log)[:, None], zl], axis=1)
    p["pcolA"] = jnp.stack([ssd_dt_bias, ssd_a_log], axis=-1)
    p["conv_w"] = ssd_conv_w
    p["conv_b"] = ssd_conv_b[:, None, :]
    p["dskA"] = jnp.repeat(ssd_d, SSD_HEAD_DIM, axis=-1)[:, None, :]
    p["nrmA"] = ssd_norm[:, None, :]

    p["prowC"] = jnp.concatenate([padv(b_igate)[:, None], padv(b_fgate)[:, None], zl], axis=1)
    z4 = jnp.zeros((DEPTH, 4), F32)
    p["pcolC"] = jnp.concatenate([b_igate, z4, b_fgate, z4], axis=1)[:, :, None]
    p["nrmC"] = ml_norm.reshape(DEPTH, 1, BR_WIDTH)

    dt = jnp.exp(s5_log_dt)[:, :, None]
    mag = jnp.exp(s5_a_re * dt)
    lr = mag * jnp.cos(s5_a_im * dt)
    li = mag * jnp.sin(s5_a_im * dt)
    den = s5_a_re * s5_a_re + s5_a_im * s5_a_im
    cr = ((lr - 1.0) * s5_a_re + li * s5_a_im) / den
    ci = (li * s5_a_re - (lr - 1.0) * s5_a_im) / den
    bb_re = cr[..., None] * s5_b_re - ci[..., None] * s5_b_im
    bb_im = cr[..., None] * s5_b_im + ci[..., None] * s5_b_re
    eye = jnp.eye(S5_CLUSTER_GROUPS, dtype=F32)
    cg = S5_CLUSTER_GROUPS

    def pack_b(bb):
        bb = bb.reshape(DEPTH, S5_CLUSTERS, cg, S5_STATE, S5_GROUP)
        return jnp.einsum("ljgnc,gh->ljgchn", bb, eye).reshape(DEPTH, S5_CLUSTERS, cg * S5_GROUP, cg * S5_STATE)

    def pack_c(cc):
        cc = cc.reshape(DEPTH, S5_CLUSTERS, cg, S5_GROUP, S5_STATE)
        return jnp.einsum("ljgcn,gh->ljgnhc", cc, eye).reshape(DEPTH, S5_CLUSTERS, cg * S5_STATE, cg * S5_GROUP)

    p["bc"] = jnp.concatenate([pack_b(bb_re), pack_b(bb_im)], axis=-1).astype(BF16)
    p["cc"] = jnp.concatenate([pack_c(s5_c_re), pack_c(-s5_c_im)], axis=-2).astype(BF16)
    p["lam"] = jnp.stack([lr.reshape(DEPTH, S5_FLAT), li.reshape(DEPTH, S5_FLAT)], axis=1)
    p["dskB"] = s5_d[:, None, :]
    p["glu_w"] = s5_glu_w.astype(BF16)
    p["glu_b"] = s5_glu_b[:, None, :]
    return p


def _group_cfg(B, T):
    if T % 512 == 0:
        return dict(Tp=None, ssd=(1, 256, 256, 2), ml=(1, 256, 256, 2), xa=(256, 2), s5=(8, 32), tm=256)
    assert T == 4 and B % 32 == 0
    return dict(Tp=True, ssd=(16, 8, T, 1), ml=(8, 16, T, 1), xa=(8, 8), s5=(32, T), tm=B * T)


def _pad_time(h, Lp):
    return jnp.pad(h, ((0, 0), (0, Lp - h.shape[1]), (0, 0)))


def _layer(p, l, x, h, mk, mv, lkv, states, ls, prev, g_next, last):
    B, T, _ = x.shape
    cfg = _group_cfg(B, T)
    padded = cfg["Tp"] is not None
    conv0, ssd0, sre0, sim0, c0, n0, m0 = states
    prev = prev or dict(a=None, b=None, c=None)

    h2d = h.reshape(B * T, D_MODEL)
    pj = (lambda w, tn: _proj_call(h2d, w, l, tn).reshape(B, T, -1)) if padded else (lambda w, tn: None)

    nb, L, valid, nsub = cfg["ssd"]
    hin = _pad_time(h, L) if padded else h
    ya, st_a = _ssd_call(hin, p, conv0, ssd0, l, ls, prev["a"], nb=nb, L=L, valid=valid, nsub=nsub,
                         proj=pj(p["wA"], WA_COLS // 3))
    nb, L, valid, nsub = cfg["ml"]
    hin = _pad_time(h, L) if padded else h
    yc, st_c = _mlstm_call(hin, p, c0, n0, m0, l, ls, prev["c"], nb=nb, L=L, valid=valid, nsub=nsub)
    if padded:
        nb, L = cfg["xa"]
        yd = _xattn_rows_call(pj(p["wD"], WD_COLS // 2), mk, mv, lkv, nb=nb, L=L)
    else:
        L, nsub = cfg["xa"]
        yd = _xattn_call(h, p["wD"], mk, mv, l, lkv, L=L, nsub=nsub)
    nb, Lt = cfg["s5"]
    yb, st_b = _s5_call(h, p, sre0, sim0, l, ls, prev["b"], nb=nb, Lt=Lt)
    new_states = dict(a=st_a, b=st_b, c=st_c)

    flat = (lambda a: a[:, :T].reshape(1, B * T, D_MODEL)) if padded else (lambda a: a)
    outs = _merge_call(flat(h), flat(ya), flat(yb), flat(yc), flat(yd), flat(x), p["wE"], p["b_gate"], p["w_down"],
                       p["w_out"], g_next, l, tm=cfg["tm"], last=last)
    x_new = outs[0].reshape(B, T, D_MODEL)
    h_next = None if last else outs[1].reshape(B, T, D_MODEL)
    return (x_new, h_next), new_states


def _first_norm(x, g):
    B, T, _ = x.shape
    cfg = _group_cfg(B, T)
    if cfg["Tp"] is None:
        return _norm_call(x, g, cfg["tm"])
    return _norm_call(x.reshape(1, B * T, D_MODEL), g, B * T).reshape(B, T, D_MODEL)


def _empty_states(B):
    z = lambda *s: jnp.zeros((DEPTH,) + s, F32)
    return dict(a=(z(B, SSD_CONV - 1, SSD_CONV_DIM), z(B, SSD_HEADS, SSD_HEAD_DIM, SSD_STATE)),
                b=(z(B * S5_GROUPS, S5_STATE), z(B * S5_GROUPS, S5_STATE)),
                c=(z(B, ML_HEADS, ML_HEAD_DIM, ML_HEAD_DIM), z(B, ML_HEADS, ML_HEAD_DIM), z(B, 1, LANES)))


def kernel(x_prompt, x_sample, mem_prompt, cache_mem_k, cache_mem_v, state_ssd_conv, state_ssd, state_s5_re,
           state_s5_im, state_mlstm_c, state_mlstm_n, state_mlstm_m, norm_in, w_in, b_gate, b_igate, b_fgate,
           ssd_conv_w, ssd_conv_b, ssd_dt_bias, ssd_a_log, ssd_d, ssd_norm, s5_a_re, s5_a_im, s5_log_dt, s5_b_re,
           s5_b_im, s5_c_re, s5_c_im, s5_d, s5_glu_w, s5_glu_b, ml_norm, mem_norm, w_mem_kv, w_down, w_out,
           final_norm):
    p = _pack_params(w_in, b_gate, b_igate, b_fgate, ssd_conv_w, ssd_conv_b, ssd_dt_bias, ssd_a_log, ssd_d,
                     ssd_norm, s5_a_re, s5_a_im, s5_log_dt, s5_b_re, s5_b_im, s5_c_re, s5_c_im, s5_d, s5_glu_w,
                     s5_glu_b, ml_norm, w_down, w_out)
    Bp, Tp, _ = x_prompt.shape
    Bs, Ts, _ = x_sample.shape

    mk_p, mv_p, mk_out, mv_out = _memkv_call(mem_prompt, mem_norm[:, None, :], w_mem_kv.astype(BF16))

    zeros = lambda *s: jnp.zeros((1,) + s, F32)
    states_p = (zeros(Bp, SSD_CONV - 1, SSD_CONV_DIM), zeros(Bp, SSD_HEADS, SSD_HEAD_DIM, SSD_STATE),
                zeros(Bp * S5_GROUPS, S5_STATE), zeros(Bp * S5_GROUPS, S5_STATE),
                zeros(Bp, ML_HEADS, ML_HEAD_DIM, ML_HEAD_DIM), zeros(Bp, ML_HEADS, ML_HEAD_DIM),
                zeros(Bp, 1, LANES))
    m_pad = jnp.pad(state_mlstm_m, ((0, 0), (0, 0), (0, LANES - ML_HEADS)))[:, :, None, :]
    states_s = (state_ssd_conv, state_ssd, state_s5_re.reshape(DEPTH, Bs * S5_GROUPS, S5_STATE),
                state_s5_im.reshape(DEPTH, Bs * S5_GROUPS, S5_STATE), state_mlstm_c, state_mlstm_n, m_pad)

    g0 = norm_in[0][None, :]
    xp, hp = x_prompt, _first_norm(x_prompt, g0)
    xs, hs = x_sample, _first_norm(x_sample, g0)
    st_p, st_s = _empty_states(Bp), _empty_states(Bs)
    for l in range(DEPTH):
        last = l == DEPTH - 1
        g_next = final_norm[None, :] if last else norm_in[l + 1][None, :]
        (xp, hp), st_p = _layer(p, l, xp, hp, mk_p, mv_p, l, states_p, 0, st_p, g_next, last)
        (xs, hs), st_s = _layer(p, l, xs, hs, cache_mem_k, cache_mem_v, l, states_s, l, st_s, g_next, last)

    def unpack(st, B):
        conv, ssd = st["a"]
        sre, sim = st["b"]
        c, n, m = st["c"]
        return (conv, ssd, sre.reshape(DEPTH, B, S5_GROUPS, S5_STATE), sim.reshape(DEPTH, B, S5_GROUPS, S5_STATE),
                c, n, m[:, :, 0, :ML_HEADS])

    return (xp, xs, mk_out, mv_out) + unpack(st_p, Bp) + unpack(st_s, Bs)
```
